```python
import jax, jax.numpy as jnp
from jax import lax
import numpy as np


D_MODEL = 4096
BATCH = 4
SEQ = 4096
DEPTH = 1

HEAD_DIM = 128
ROPE_THETA = 10000.0
EPS = 1e-6
Q_BLOCK = 128
DIL_PATTERNS = ((128, 1), (512, 4), (2048, 16))
N_DIL_GROUPS = 3
A_HEADS = 8
A_WIDTH = A_HEADS * HEAD_DIM
A_QKV_WIDTH = N_DIL_GROUPS * A_WIDTH
B_HEADS = 16
B_WIDTH = B_HEADS * HEAD_DIM
A_Q0 = 0
A_K0 = A_Q0 + A_QKV_WIDTH
A_V0 = A_K0 + A_QKV_WIDTH
A_G0 = A_V0 + A_QKV_WIDTH
B_Q0 = A_G0 + A_WIDTH
B_K0 = B_Q0 + B_WIDTH
B_V0 = B_K0 + B_WIDTH
B_G0 = B_V0 + B_WIDTH
B_F0 = B_G0 + B_WIDTH
M_G0 = B_F0 + B_HEADS
IN_COLS = M_G0 + 2 * D_MODEL
FORGET_BIAS = 3.0

kernel_name = 'hybrid_dilated_forgetting_attention_block'


def rms_norm(x, g):
    xf = x.astype(jnp.float32)
    y = xf * lax.rsqrt(jnp.mean(xf * xf, axis=-1, keepdims=True) + EPS)
    return (y * g.astype(jnp.float32)).astype(x.dtype)


def rope(x, pos):
    half = HEAD_DIM // 2
    inv = ROPE_THETA ** (-jnp.arange(half, dtype=jnp.float32) / half)
    ang = pos.astype(jnp.float32)[:, None] * inv[None, :]
    bshape = (ang.shape[0],) + (1,) * (x.ndim - 3) + (half,)
    cos = jnp.cos(ang).reshape(bshape)
    sin = jnp.sin(ang).reshape(bshape)
    x1 = x[..., :half].astype(jnp.float32)
    x2 = x[..., half:].astype(jnp.float32)
    out = jnp.concatenate([x1 * cos - x2 * sin, x2 * cos + x1 * sin], axis=-1)
    return out.astype(x.dtype)


def dilated_window_attention(q, k, v, window, dilation):
    b, s, h, dh = q.shape
    span = window // dilation
    L = s // dilation
    n = -(-L // span)
    Lp = n * span

    def to_strided(t):
        t = t.reshape(b, L, dilation, h, dh).transpose(0, 2, 1, 3, 4)
        return jnp.pad(t, ((0, 0), (0, 0), (0, Lp - L), (0, 0), (0, 0)))

    def band(t):
        t = jnp.pad(t, ((0, 0), (0, 0), (span, 0), (0, 0), (0, 0)))
        t = t.reshape(b, dilation, n + 1, span, h, dh)
        return jnp.concatenate([t[:, :, :-1], t[:, :, 1:]], axis=3)

    qb = to_strided(q).reshape(b, dilation, n, span, h, dh)
    kb = band(to_strided(k))
    vb = band(to_strided(v))
    scores = jnp.einsum('brnqhd,brnkhd->brnhqk', qb.astype(jnp.float32),
                        kb.astype(jnp.float32)) * (HEAD_DIM ** -0.5)
    qi = jnp.arange(span)[:, None]
    kj = jnp.arange(2 * span)[None, :]
    dist = qi - kj + span
    blk = jnp.arange(n)[:, None, None]
    valid = (dist >= 0) & (dist <= span) & ((blk > 0) | (kj >= span))
    scores = jnp.where(valid[None, None, :, None], scores, -jnp.inf)
    m = jnp.max(scores, axis=-1, keepdims=True)
    p = jnp.exp(scores - m)
    den = jnp.sum(p, axis=-1, keepdims=True)
    o = jnp.einsum('brnhqk,brnkhd->brnqhd', (p / den).astype(v.dtype), vb)
    lse = (m + jnp.log(den))[..., 0].transpose(0, 1, 2, 4, 3)

    def from_strided(t):
        rest = t.shape[4:]
        t = t.reshape((b, dilation, Lp) + rest)[:, :, :L]
        return jnp.moveaxis(t, 1, 2).reshape((b, s) + rest)

    return from_strided(o), from_strided(lse)


def forgetting_attention(q, k, v, log_f):
    b, s, h, dh = q.shape
    nq = s // Q_BLOCK
    F = jnp.cumsum(log_f.astype(jnp.float32), axis=1)
    Ft = F.transpose(0, 2, 1)
    kf = k.astype(jnp.float32)
    kpos = jnp.arange(s)
    qb = q.reshape(b, nq, Q_BLOCK, h, dh).transpose(1, 0, 2, 3, 4)
    Fb = Ft.reshape(b, h, nq, Q_BLOCK).transpose(2, 0, 1, 3)

    def block(args):
        qi, Fi, i = args
        sc = jnp.einsum('bqhd,bkhd->bhqk', qi.astype(jnp.float32), kf) * (HEAD_DIM ** -0.5)
        sc = sc + Fi[..., None] - Ft[:, :, None, :]
        qpos = i * Q_BLOCK + jnp.arange(Q_BLOCK)
        sc = jnp.where((kpos[None, :] <= qpos[:, None])[None, None], sc, -jnp.inf)
        p = jax.nn.softmax(sc, axis=-1)
        return jnp.einsum('bhqk,bkhd->bqhd', p.astype(v.dtype), v)

    o = lax.map(block, (qb, Fb, jnp.arange(nq)))
    return o.transpose(1, 0, 2, 3, 4).reshape(b, s, h, dh)


def setup_inputs(seed: int = 0) -> dict:
    key = jax.random.key(seed)
    ks = jax.random.split(key, 16)
    f32 = jnp.float32
    x = jax.random.normal(ks[0], (BATCH, SEQ, D_MODEL), f32)
    c = jax.random.normal(ks[1], (BATCH, D_MODEL), f32)
    norm_g = 1.0 + 0.02 * jax.random.normal(ks[2], (DEPTH, D_MODEL), f32)
    w_ada = jax.random.normal(ks[3], (DEPTH, D_MODEL, 3 * D_MODEL), f32) * (0.5 * D_MODEL ** -0.5)
    b_ada = 0.01 * jax.random.normal(ks[4], (DEPTH, 3 * D_MODEL), f32)
    w_in = jax.random.normal(ks[5], (DEPTH, D_MODEL, IN_COLS), f32) * (D_MODEL ** -0.5)
    b_in = 0.01 * jax.random.normal(ks[6], (DEPTH, IN_COLS), f32)
    b_in = b_in.at[:, B_F0:M_G0].add(FORGET_BIAS)
    a_q_norm = 1.0 + 0.02 * jax.random.normal(ks[7], (DEPTH, N_DIL_GROUPS, HEAD_DIM), f32)
    a_k_norm = 1.0 + 0.02 * jax.random.normal(ks[8], (DEPTH, N_DIL_GROUPS, HEAD_DIM), f32)
    b_q_norm = 1.0 + 0.02 * jax.random.normal(ks[9], (DEPTH, HEAD_DIM), f32)
    b_k_norm = 1.0 + 0.02 * jax.random.normal(ks[10], (DEPTH, HEAD_DIM), f32)
    w_a_out = jax.random.normal(ks[11], (DEPTH, A_WIDTH, D_MODEL), f32) * (A_WIDTH ** -0.5)
    w_b_out = jax.random.normal(ks[12], (DEPTH, B_WIDTH, D_MODEL), f32) * (B_WIDTH ** -0.5)
    w_o = jax.random.normal(ks[13], (DEPTH, D_MODEL, D_MODEL), f32) * (D_MODEL ** -0.5)
    return {'x': x, 'c': c, 'norm_g': norm_g, 'w_ada': w_ada, 'b_ada': b_ada,
            'w_in': w_in, 'b_in': b_in, 'a_q_norm': a_q_norm, 'a_k_norm': a_k_norm,
            'b_q_norm': b_q_norm, 'b_k_norm': b_k_norm, 'w_a_out': w_a_out,
            'w_b_out': w_b_out, 'w_o': w_o}


def reference(x, c, norm_g, w_ada, b_ada, w_in, b_in, a_q_norm, a_k_norm,
              b_q_norm, b_k_norm, w_a_out, w_b_out, w_o):
    b, s, _ = x.shape
    pos = jnp.arange(s)
    for l in range(DEPTH):
        mod = jax.nn.silu(c) @ w_ada[l] + b_ada[l]
        shift, scale, gate = jnp.split(mod, 3, axis=-1)
        h = rms_norm(x, norm_g[l]) * (1.0 + scale[:, None, :]) + shift[:, None, :]
        z = h @ w_in[l] + b_in[l]

        qa = z[..., A_Q0:A_K0].reshape(b, s, N_DIL_GROUPS, A_HEADS, HEAD_DIM)
        ka = z[..., A_K0:A_V0].reshape(b, s, N_DIL_GROUPS, A_HEADS, HEAD_DIM)
        va = z[..., A_V0:A_G0].reshape(b, s, N_DIL_GROUPS, A_HEADS, HEAD_DIM)
        qa = rope(rms_norm(qa, a_q_norm[l][:, None, :]), pos)
        ka = rope(rms_norm(ka, a_k_norm[l][:, None, :]), pos)
        outs = []
        lses = []
        for g, (win, dil) in enumerate(DIL_PATTERNS):
            o_g, lse_g = dilated_window_attention(qa[:, :, g], ka[:, :, g], va[:, :, g], win, dil)
            outs.append(o_g)
            lses.append(lse_g)
        wts = jax.nn.softmax(jnp.stack(lses, axis=0), axis=0)
        ya = jnp.sum(wts[..., None] * jnp.stack(outs, axis=0).astype(jnp.float32), axis=0)
        ya = ya.astype(x.dtype).reshape(b, s, A_WIDTH) * jax.nn.silu(z[..., A_G0:B_Q0])

        qb = rms_norm(z[..., B_Q0:B_K0].reshape(b, s, B_HEADS, HEAD_DIM), b_q_norm[l])
        kb = rms_norm(z[..., B_K0:B_V0].reshape(b, s, B_HEADS, HEAD_DIM), b_k_norm[l])
        vb = z[..., B_V0:B_G0].reshape(b, s, B_HEADS, HEAD_DIM)
        log_f = jax.nn.log_sigmoid(z[..., B_F0:M_G0].astype(jnp.float32))
        yb = forgetting_attention(qb, kb, vb, log_f).reshape(b, s, B_WIDTH)
        yb = yb * jax.nn.silu(z[..., B_G0:B_F0])

        ga = jax.nn.sigmoid(z[..., M_G0:M_G0 + D_MODEL])
        gb = jax.nn.sigmoid(z[..., M_G0 + D_MODEL:])
        merged = ga * (ya @ w_a_out[l]) + gb * (yb @ w_b_out[l])
        out = merged @ w_o[l]
        x = x + gate[:, None, :] * out
    return x
```

```python
import functools

import jax
import jax.numpy as jnp
from jax import lax
from jax.experimental import pallas as pl
from jax.experimental.pallas import tpu as pltpu

F32 = jnp.float32
BF16 = jnp.bfloat16

D_MODEL = 4096
BATCH = 4
SEQ = 4096
HEAD_DIM = 128
ROPE_THETA = 10000.0
EPS = 1e-6
DIL_PATTERNS = ((128, 1), (512, 4), (2048, 16))
N_DIL_GROUPS = 3
A_HEADS = 8
A_WIDTH = A_HEADS * HEAD_DIM
A_QKV_WIDTH = N_DIL_GROUPS * A_WIDTH
B_HEADS = 16
B_WIDTH = B_HEADS * HEAD_DIM
A_Q0 = 0
A_K0 = A_Q0 + A_QKV_WIDTH
A_V0 = A_K0 + A_QKV_WIDTH
A_G0 = A_V0 + A_QKV_WIDTH
B_Q0 = A_G0 + A_WIDTH
B_K0 = B_Q0 + B_WIDTH
B_V0 = B_K0 + B_WIDTH
B_G0 = B_V0 + B_WIDTH
B_F0 = B_G0 + B_WIDTH
M_G0 = B_F0 + B_HEADS
IN_COLS = M_G0 + 2 * D_MODEL
N_TOK = BATCH * SEQ
SPAN = 128
SCALE = HEAD_DIM ** -0.5
LANES = 128
VMEM_LIMIT = 56 * 1024 * 1024


def _params(sem, vmem=VMEM_LIMIT):
    return pltpu.CompilerParams(dimension_semantics=sem, vmem_limit_bytes=vmem)


def _sigmoid(x):
    return 1.0 / (1.0 + jnp.exp(-x))


def _silu(x):
    return x * _sigmoid(x)


def _mod_kernel(c_ref, w_ref, b_ref, o_ref):
    s = _silu(c_ref[...]).astype(BF16)
    o_ref[...] = jnp.dot(s, w_ref[...].astype(BF16), preferred_element_type=F32) + b_ref[...]


def _modulation(c, w_ada, b_ada):
    tn = 512
    n = w_ada.shape[1]
    cp = jnp.pad(c, ((0, 8 - BATCH), (0, 0)))
    out = pl.pallas_call(
        _mod_kernel,
        grid=(n // tn,),
        in_specs=[pl.BlockSpec((8, D_MODEL), lambda j: (0, 0)),
                  pl.BlockSpec((D_MODEL, tn), lambda j: (0, j)),
                  pl.BlockSpec((1, tn), lambda j: (0, j))],
        out_specs=pl.BlockSpec((8, tn), lambda j: (0, j)),
        out_shape=jax.ShapeDtypeStruct((8, n), F32),
        compiler_params=_params(("arbitrary",)),
        name="modulation",
    )(cp, w_ada, b_ada.reshape(1, n))
    return out[:BATCH]


def _norm_kernel(x_ref, g_ref, scale_ref, shift_ref, h_ref):
    x = x_ref[...]
    ms = jnp.mean(x * x, axis=-1, keepdims=True)
    y = x * lax.rsqrt(ms + EPS) * g_ref[...]
    h_ref[...] = (y * (1.0 + scale_ref[...]) + shift_ref[...]).astype(BF16)


def _norm_modulate(x2, g, scale, shift):
    tm = 512
    per_b = SEQ // tm
    return pl.pallas_call(
        _norm_kernel,
        grid=(N_TOK // tm,),
        in_specs=[pl.BlockSpec((tm, D_MODEL), lambda i: (i, 0)),
                  pl.BlockSpec((1, D_MODEL), lambda i: (0, 0)),
                  pl.BlockSpec((None, 1, D_MODEL), lambda i: (i // per_b, 0, 0)),
                  pl.BlockSpec((None, 1, D_MODEL), lambda i: (i // per_b, 0, 0))],
        out_specs=pl.BlockSpec((tm, D_MODEL), lambda i: (i, 0)),
        out_shape=jax.ShapeDtypeStruct((N_TOK, D_MODEL), BF16),
        compiler_params=_params(("arbitrary",)),
        name="norm_modulate",
    )(x2, g.reshape(1, D_MODEL), scale.reshape(BATCH, 1, D_MODEL), shift.reshape(BATCH, 1, D_MODEL))


def _proj_kernel(h_ref, w_ref, b_ref, o_ref, *, act):
    z = jnp.dot(h_ref[...], w_ref[...], preferred_element_type=F32) + b_ref[...]
    if act == "sigmoid":
        z = _sigmoid(z)
    o_ref[...] = z.astype(o_ref.dtype)


def _project(h, w, b, out_dtype, act, name, tm=1024, tn=1024):
    m, k = h.shape
    n = w.shape[1]
    return pl.pallas_call(
        functools.partial(_proj_kernel, act=act),
        grid=(m // tm, n // tn),
        in_specs=[pl.BlockSpec((tm, k), lambda i, j: (i, 0)),
                  pl.BlockSpec((k, tn), lambda i, j: (0, j)),
                  pl.BlockSpec((1, tn), lambda i, j: (0, j))],
        out_specs=pl.BlockSpec((tm, tn), lambda i, j: (i, j)),
        out_shape=jax.ShapeDtypeStruct((m, n), out_dtype),
        compiler_params=_params(("arbitrary", "arbitrary")),
        name=name,
    )(h, w, b.reshape(1, n))


def _forget_kernel(h_ref, w_ref, b_ref, f_ref, carry_ref, *, tc):
    @pl.when(pl.program_id(1) == 0)
    def _():
        carry_ref[...] = jnp.zeros_like(carry_ref)

    z = jnp.dot(h_ref[...], w_ref[...], preferred_element_type=F32) + b_ref[...]
    x = jnp.minimum(z, 0.0) - jnp.log1p(jnp.exp(-jnp.abs(z)))
    row = lax.broadcasted_iota(jnp.int32, x.shape, 0)
    shift = 1
    while shift < tc:
        x = x + jnp.where(row >= shift, pltpu.roll(x, shift, 0), 0.0)
        shift *= 2
    x = x + carry_ref[...]
    f_ref[...] = x
    carry_ref[...] = x[tc - 1:tc, :]


def _forget_cumsum(h, w_f, b_f):
    tc = 512
    per_b = SEQ // tc
    return pl.pallas_call(
        functools.partial(_forget_kernel, tc=tc),
        grid=(BATCH, per_b),
        in_specs=[pl.BlockSpec((tc, D_MODEL), lambda b, c: (b * per_b + c, 0)),
                  pl.BlockSpec((D_MODEL, LANES), lambda b, c: (0, 0)),
                  pl.BlockSpec((1, LANES), lambda b, c: (0, 0))],
        out_specs=pl.BlockSpec((tc, LANES), lambda b, c: (b * per_b + c, 0)),
        out_shape=jax.ShapeDtypeStruct((N_TOK, LANES), F32),
        scratch_shapes=[pltpu.VMEM((1, LANES), F32)],
        compiler_params=_params(("arbitrary", "arbitrary")),
        name="forget_cumsum",
    )(h, w_f, b_f)


def _attn_a_kernel(q_ref, k_ref, v_ref, cos_ref, sin_ref, qg_ref, kg_ref, o_ref, lse_ref,
                   q_s, k_s, v_s, *, length):
    cos = cos_ref[...]
    sin = sin_ref[...]

    def prep(x, g):
        ms = jnp.mean(x * x, axis=-1, keepdims=True)
        y = x * lax.rsqrt(ms + EPS) * g
        return y * cos + pltpu.roll(y, HEAD_DIM // 2, 1) * sin

    q_s[...] = prep(q_ref[...], qg_ref[...]).astype(BF16)
    k_s[pl.ds(0, SPAN), :] = jnp.zeros((SPAN, HEAD_DIM), BF16)
    v_s[pl.ds(0, SPAN), :] = jnp.zeros((SPAN, HEAD_DIM), BF16)
    k_s[pl.ds(SPAN, length), :] = prep(k_ref[...], kg_ref[...]).astype(BF16)
    v_s[pl.ds(SPAN, length), :] = v_ref[...].astype(BF16)

    qi = lax.broadcasted_iota(jnp.int32, (SPAN, 2 * SPAN), 0)
    kj = lax.broadcasted_iota(jnp.int32, (SPAN, 2 * SPAN), 1)
    dist = qi - kj + SPAN
    band = (dist >= 0) & (dist <= SPAN)
    cur = kj >= SPAN

    def body(blk, carry):
        off = pl.multiple_of(blk * SPAN, SPAN)
        qb = q_s[pl.ds(off, SPAN), :]
        kb = k_s[pl.ds(off, 2 * SPAN), :]
        vb = v_s[pl.ds(off, 2 * SPAN), :]
        s = lax.dot_general(qb, kb, (((1,), (1,)), ((), ())), preferred_element_type=F32) * SCALE
        valid = band & (cur | (blk > 0))
        s = jnp.where(valid, s, -jnp.inf)
        m = jnp.max(s, axis=-1, keepdims=True)
        p = jnp.exp(s - m)
        den = jnp.sum(p, axis=-1, keepdims=True)
        o = jnp.dot((p / den).astype(BF16), vb, preferred_element_type=F32)
        o_ref[pl.ds(off, SPAN), :] = o
        lse_ref[pl.ds(off, SPAN), :] = jnp.broadcast_to(m + jnp.log(den), (SPAN, HEAD_DIM))
        return carry

    lax.fori_loop(0, length // SPAN, body, 0)


def _attn_a_group(z_main, cos_t, sin_t, qg, kg, g, dil):
    length = SEQ // dil
    zc = z_main.shape[1]
    cb = zc // HEAD_DIM
    zv = z_main.reshape(BATCH, length, dil * zc)
    cosv = cos_t.reshape(length, dil * HEAD_DIM)
    sinv = sin_t.reshape(length, dil * HEAD_DIM)
    q0 = A_Q0 // HEAD_DIM + g * A_HEADS
    k0 = A_K0 // HEAD_DIM + g * A_HEADS
    v0 = A_V0 // HEAD_DIM + g * A_HEADS

    def zspec(c0):
        return pl.BlockSpec((None, length, HEAD_DIM), lambda b, r, h: (b, 0, r * cb + c0 + h))

    tspec = pl.BlockSpec((length, HEAD_DIM), lambda b, r, h: (0, r))
    gspec = pl.BlockSpec((1, HEAD_DIM), lambda b, r, h: (0, 0))
    ospec = pl.BlockSpec((None, length, HEAD_DIM), lambda b, r, h: (b, 0, r * A_HEADS + h))
    oshape = jax.ShapeDtypeStruct((BATCH, length, dil * A_WIDTH), F32)
    o, lse = pl.pallas_call(
        functools.partial(_attn_a_kernel, length=length),
        grid=(BATCH, dil, A_HEADS),
        in_specs=[zspec(q0), zspec(k0), zspec(v0), tspec, tspec, gspec, gspec],
        out_specs=[ospec, ospec],
        out_shape=[oshape, oshape],
        scratch_shapes=[pltpu.VMEM((length, HEAD_DIM), BF16),
                        pltpu.VMEM((length + SPAN, HEAD_DIM), BF16),
                        pltpu.VMEM((length + SPAN, HEAD_DIM), BF16)],
        compiler_params=_params(("arbitrary", "arbitrary", "arbitrary")),
        name=f"attn_a_d{dil}",
    )(zv, zv, zv, cosv, sinv, qg.reshape(1, HEAD_DIM), kg.reshape(1, HEAD_DIM))
    return o.reshape(N_TOK, A_WIDTH), lse.reshape(N_TOK, A_WIDTH)


def _combine_kernel(o0, o1, o2, l0, l1, l2, gate_ref, ya_ref):
    a, b, c = l0[...], l1[...], l2[...]
    m = jnp.maximum(jnp.maximum(a, b), c)
    ea, eb, ec = jnp.exp(a - m), jnp.exp(b - m), jnp.exp(c - m)
    inv = 1.0 / (ea + eb + ec)
    ya = (ea * inv) * o0[...] + (eb * inv) * o1[...] + (ec * inv) * o2[...]
    ya_ref[...] = (ya * _silu(gate_ref[...])).astype(BF16)


def _combine_a(outs, lses, z_main):
    tm = 512
    spec = pl.BlockSpec((tm, A_WIDTH), lambda i: (i, 0))
    gspec = pl.BlockSpec((tm, A_WIDTH), lambda i: (i, A_G0 // A_WIDTH))
    return pl.pallas_call(
        _combine_kernel,
        grid=(N_TOK // tm,),
        in_specs=[spec] * 6 + [gspec],
        out_specs=spec,
        out_shape=jax.ShapeDtypeStruct((N_TOK, A_WIDTH), BF16),
        compiler_params=_params(("arbitrary",)),
        name="combine_a",
    )(*outs, *lses, z_main)


def _fox_kernel(q_ref, k_ref, v_ref, gate_ref, fq_ref, fk_ref, qg_ref, kg_ref, y_ref,
                k_s, v_s, *, tq):
    iq = pl.program_id(2)

    @pl.when(iq == 0)
    def _():
        k = k_ref[...]
        ms = jnp.mean(k * k, axis=-1, keepdims=True)
        k_s[...] = (k * lax.rsqrt(ms + EPS) * kg_ref[...]).astype(BF16)
        v_s[...] = v_ref[...].astype(BF16)

    q = q_ref[...]
    ms = jnp.mean(q * q, axis=-1, keepdims=True)
    qn = (q * lax.rsqrt(ms + EPS) * qg_ref[...]).astype(BF16)
    fq = fq_ref[...]

    def kv_step(j, carry, diagonal):
        m, l, acc = carry
        off = pl.multiple_of(j * tq, tq)
        kb = k_s[pl.ds(off, tq), :]
        vb = v_s[pl.ds(off, tq), :]
        s = lax.dot_general(qn, kb, (((1,), (1,)), ((), ())), preferred_element_type=F32) * SCALE
        s = s + fq - fk_ref[:, pl.ds(off, tq)]
        if diagonal:
            row = lax.broadcasted_iota(jnp.int32, (tq, tq), 0)
            col = lax.broadcasted_iota(jnp.int32, (tq, tq), 1)
            s = jnp.where(col <= row, s, -jnp.inf)
        m_new = jnp.maximum(m, jnp.max(s, axis=-1, keepdims=True))
        alpha = jnp.exp(m - m_new)
        p = jnp.exp(s - m_new)
        l = alpha * l + jnp.sum(p, axis=-1, keepdims=True)
        acc = alpha * acc + jnp.dot(p.astype(BF16), vb, preferred_element_type=F32)
        return m_new, l, acc

    init = (jnp.full((tq, 1), -jnp.inf, F32), jnp.zeros((tq, 1), F32), jnp.zeros((tq, HEAD_DIM), F32))
    carry = lax.fori_loop(0, iq, lambda j, c: kv_step(j, c, False), init)
    _, l, acc = kv_step(iq, carry, True)
    y_ref[...] = ((acc / l) * _silu(gate_ref[...])).astype(BF16)


def _fox_attention(z_main, f_col, f_row, qg, kg):
    tq = 512
    zv = z_main.reshape(BATCH, SEQ, z_main.shape[1])
    q0, k0, v0, g0 = (c // HEAD_DIM for c in (B_Q0, B_K0, B_V0, B_G0))
    qspec = lambda c0: pl.BlockSpec((None, tq, HEAD_DIM), lambda b, h, i: (b, i, c0 + h))
    kspec = lambda c0: pl.BlockSpec((None, SEQ, HEAD_DIM), lambda b, h, i: (b, 0, c0 + h))
    nspec = pl.BlockSpec((1, HEAD_DIM), lambda b, h, i: (0, 0))
    return pl.pallas_call(
        functools.partial(_fox_kernel, tq=tq),
        grid=(BATCH, B_HEADS, SEQ // tq),
        in_specs=[qspec(q0), kspec(k0), kspec(v0), qspec(g0),
                  pl.BlockSpec((None, None, tq, 1), lambda b, h, i: (b, h, i, 0)),
                  pl.BlockSpec((None, None, 1, SEQ), lambda b, h, i: (b, h, 0, 0)),
                  nspec, nspec],
        out_specs=pl.BlockSpec((None, tq, HEAD_DIM), lambda b, h, i: (b, i, h)),
        out_shape=jax.ShapeDtypeStruct((BATCH, SEQ, B_WIDTH), BF16),
        scratch_shapes=[pltpu.VMEM((SEQ, HEAD_DIM), BF16), pltpu.VMEM((SEQ, HEAD_DIM), BF16)],
        compiler_params=_params(("arbitrary", "arbitrary", "arbitrary")),
        name="fox_attention",
    )(zv, zv, zv, zv, f_col, f_row, qg.reshape(1, HEAD_DIM), kg.reshape(1, HEAD_DIM))


def _merge_kernel(ya_ref, yb_ref, wa_ref, wb_ref, ga_ref, gb_ref, o_ref):
    pa = jnp.dot(ya_ref[...], wa_ref[...], preferred_element_type=F32)
    pb = jnp.dot(yb_ref[...], wb_ref[...], preferred_element_type=F32)
    o_ref[...] = (ga_ref[...].astype(F32) * pa + gb_ref[...].astype(F32) * pb).astype(BF16)


def _merge(ya, yb, wa, wb, gates):
    tm, tn = 1024, 1024
    nb = D_MODEL // tn
    return pl.pallas_call(
        _merge_kernel,
        grid=(N_TOK // tm, nb),
        in_specs=[pl.BlockSpec((tm, A_WIDTH), lambda i, j: (i, 0)),
                  pl.BlockSpec((tm, B_WIDTH), lambda i, j: (i, 0)),
                  pl.BlockSpec((A_WIDTH, tn), lambda i, j: (0, j)),
                  pl.BlockSpec((B_WIDTH, tn), lambda i, j: (0, j)),
                  pl.BlockSpec((tm, tn), lambda i, j: (i, j)),
                  pl.BlockSpec((tm, tn), lambda i, j: (i, nb + j))],
        out_specs=pl.BlockSpec((tm, tn), lambda i, j: (i, j)),
        out_shape=jax.ShapeDtypeStruct((N_TOK, D_MODEL), BF16),
        compiler_params=_params(("arbitrary", "arbitrary")),
        name="merge",
    )(ya, yb, wa, wb, gates, gates)


def _out_kernel(m_ref, w_ref, x_ref, gate_ref, o_ref):
    out = jnp.dot(m_ref[...], w_ref[...], preferred_element_type=F32)
    o_ref[...] = x_ref[...] + gate_ref[...] * out


def _out_proj(merged, w_o, x2, gate):
    tm, tn = 1024, 512
    per_b = SEQ // tm
    return pl.pallas_call(
        _out_kernel,
        grid=(N_TOK // tm, D_MODEL // tn),
        in_specs=[pl.BlockSpec((tm, D_MODEL), lambda i, j: (i, 0)),
                  pl.BlockSpec((D_MODEL, tn), lambda i, j: (0, j)),
                  pl.BlockSpec((tm, tn), lambda i, j: (i, j)),
                  pl.BlockSpec((None, 1, tn), lambda i, j: (i // per_b, 0, j))],
        out_specs=pl.BlockSpec((tm, tn), lambda i, j: (i, j)),
        out_shape=jax.ShapeDtypeStruct((N_TOK, D_MODEL), F32),
        compiler_params=_params(("arbitrary", "arbitrary")),
        name="out_proj",
    )(merged, w_o, x2, gate.reshape(BATCH, 1, D_MODEL))


def _rope_tables():
    half = HEAD_DIM // 2
    inv = ROPE_THETA ** (-jnp.arange(half, dtype=F32) / half)
    ang = jnp.arange(SEQ).astype(F32)[:, None] * inv[None, :]
    cos, sin = jnp.cos(ang), jnp.sin(ang)
    return jnp.concatenate([cos, cos], axis=-1), jnp.concatenate([-sin, sin], axis=-1)


def kernel(x, c, norm_g, w_ada, b_ada, w_in, b_in, a_q_norm, a_k_norm, b_q_norm, b_k_norm,
           w_a_out, w_b_out, w_o):
    depth = norm_g.shape[0]
    cos_t, sin_t = _rope_tables()
    xs = x.reshape(N_TOK, D_MODEL)
    for l in range(depth):
        mod = _modulation(c, w_ada[l], b_ada[l])
        shift, scale, gate = mod[:, :D_MODEL], mod[:, D_MODEL:2 * D_MODEL], mod[:, 2 * D_MODEL:]
        h = _norm_modulate(xs, norm_g[l], scale, shift)

        w_bf = w_in[l].astype(BF16)
        z_main = _project(h, w_bf[:, :B_F0], b_in[l, :B_F0], F32, None, "proj_main")
        gates = _project(h, w_bf[:, M_G0:], b_in[l, M_G0:], BF16, "sigmoid", "proj_gates")
        w_f = jnp.pad(w_bf[:, B_F0:M_G0], ((0, 0), (0, LANES - B_HEADS)))
        b_f = jnp.pad(b_in[l, B_F0:M_G0], (0, LANES - B_HEADS)).reshape(1, LANES)
        f_cum = _forget_cumsum(h, w_f, b_f)

        outs, lses = [], []
        for g, (_, dil) in enumerate(DIL_PATTERNS):
            o_g, lse_g = _attn_a_group(z_main, cos_t, sin_t, a_q_norm[l, g], a_k_norm[l, g], g, dil)
            outs.append(o_g)
            lses.append(lse_g)
        ya = _combine_a(outs, lses, z_main)

        f_bhs = jnp.transpose(f_cum.reshape(BATCH, SEQ, LANES)[:, :, :B_HEADS], (0, 2, 1))
        yb = _fox_attention(z_main, f_bhs[..., None], f_bhs[:, :, None, :], b_q_norm[l], b_k_norm[l])

        merged = _merge(ya, yb.reshape(N_TOK, B_WIDTH), w_a_out[l].astype(BF16),
                        w_b_out[l].astype(BF16), gates)
        xs = _out_proj(merged, w_o[l].astype(BF16), xs, gate)
    return xs.reshape(BATCH, SEQ, D_MODEL)
```

```python
import functools
import math

import jax
import jax.numpy as jnp
from jax import lax
from jax.experimental import pallas as pl
from jax.experimental.pallas import tpu as pltpu

F32 = jnp.float32
BF16 = jnp.bfloat16

D_MODEL = 4096
BATCH = 4
SEQ = 4096
HEAD_DIM = 128
ROPE_THETA = 10000.0
EPS = 1e-6
DIL_PATTERNS = ((128, 1), (512, 4), (2048, 16))
DILS = tuple(d for _, d in DIL_PATTERNS)
N_DIL_GROUPS = 3
A_HEADS = 8
A_WIDTH = A_HEADS * HEAD_DIM
A_QKV_WIDTH = N_DIL_GROUPS * A_WIDTH
B_HEADS = 16
B_WIDTH = B_HEADS * HEAD_DIM
A_Q0 = 0
A_K0 = A_Q0 + A_QKV_WIDTH
A_V0 = A_K0 + A_QKV_WIDTH
A_G0 = A_V0 + A_QKV_WIDTH
B_Q0 = A_G0 + A_WIDTH
B_K0 = B_Q0 + B_WIDTH
B_V0 = B_K0 + B_WIDTH
B_G0 = B_V0 + B_WIDTH
B_F0 = B_G0 + B_WIDTH
M_G0 = B_F0 + B_HEADS
IN_COLS = M_G0 + 2 * D_MODEL
N_TOK = BATCH * SEQ
SPAN = 128
LANES = 128
VMEM_LIMIT = 56 * 1024 * 1024
PROJ_TM = 1024
PROJ_TN = 1024
LOG2E = math.log2(math.e)
C1 = (HEAD_DIM ** -0.5) * LOG2E


def _params(sem, vmem=VMEM_LIMIT):
    return pltpu.CompilerParams(dimension_semantics=sem, vmem_limit_bytes=vmem)


def _sigmoid(x):
    return 1.0 / (1.0 + jnp.exp(-x))


def _silu(x):
    return x * _sigmoid(x)


def _mod_kernel(c_ref, w_ref, b_ref, o_ref):
    s = _silu(c_ref[...]).astype(BF16)
    o_ref[...] = jnp.dot(s, w_ref[...].astype(BF16), preferred_element_type=F32) + b_ref[...]


def _modulation(c, w_ada, b_ada):
    tn = 512
    n = w_ada.shape[1]
    cp = jnp.pad(c, ((0, 8 - BATCH), (0, 0)))
    out = pl.pallas_call(
        _mod_kernel,
        grid=(n // tn,),
        in_specs=[pl.BlockSpec((8, D_MODEL), lambda j: (0, 0)),
                  pl.BlockSpec((D_MODEL, tn), lambda j: (0, j)),
                  pl.BlockSpec((1, tn), lambda j: (0, j))],
        out_specs=pl.BlockSpec((8, tn), lambda j: (0, j)),
        out_shape=jax.ShapeDtypeStruct((8, n), F32),
        compiler_params=_params(("arbitrary",)),
        name="modulation",
    )(cp, w_ada, b_ada.reshape(1, n))
    return out[:BATCH]


def _norm_kernel(x_ref, g_ref, scale_ref, shift_ref, h_ref):
    x = x_ref[...]
    ms = jnp.mean(x * x, axis=-1, keepdims=True)
    y = x * lax.rsqrt(ms + EPS) * g_ref[...]
    h_ref[...] = (y * (1.0 + scale_ref[...]) + shift_ref[...]).astype(BF16)


def _norm_modulate(x2, g, scale, shift):
    tm = 512
    per_b = SEQ // tm
    return pl.pallas_call(
        _norm_kernel,
        grid=(N_TOK // tm,),
        in_specs=[pl.BlockSpec((tm, D_MODEL), lambda i: (i, 0)),
                  pl.BlockSpec((1, D_MODEL), lambda i: (0, 0)),
                  pl.BlockSpec((None, 1, D_MODEL), lambda i: (i // per_b, 0, 0)),
                  pl.BlockSpec((None, 1, D_MODEL), lambda i: (i // per_b, 0, 0))],
        out_specs=pl.BlockSpec((tm, D_MODEL), lambda i: (i, 0)),
        out_shape=jax.ShapeDtypeStruct((N_TOK, D_MODEL), BF16),
        compiler_params=_params(("arbitrary",)),
        name="norm_modulate",
    )(x2, g.reshape(1, D_MODEL), scale.reshape(BATCH, 1, D_MODEL), shift.reshape(BATCH, 1, D_MODEL))


def _head_proj_kernel(h_ref, w_ref, b_ref, gain_ref, cos_ref, sin_ref, o_ref, perm_s, *, plan):
    j = pl.program_id(1)
    tm = h_ref.shape[0]
    heads = w_ref.shape[1] // HEAD_DIM

    def epilogue(x, hh, kind, rows):
        if kind in ("rope", "norm"):
            ms = jnp.mean(x * x, axis=-1, keepdims=True)
            x = x * lax.rsqrt(ms + EPS) * gain_ref[:, hh * HEAD_DIM:(hh + 1) * HEAD_DIM]
            if kind == "rope":
                x = x * cos_ref[rows, :] + pltpu.roll(x, HEAD_DIM // 2, 1) * sin_ref[rows, :]
        elif kind == "silu":
            x = _silu(x)
        return x.astype(BF16)

    def run(kind, dil):
        acc = jnp.dot(h_ref[...], w_ref[...], preferred_element_type=F32) + b_ref[...]
        for hh in range(heads):
            x = acc[:, hh * HEAD_DIM:(hh + 1) * HEAD_DIM]
            if dil == 1:
                o_ref[hh] = epilogue(x, hh, kind, slice(None))
            else:
                perm_s[hh * tm:(hh + 1) * tm, :] = x
                n = tm // dil
                for r in range(dil):
                    rows = slice(r * n, (r + 1) * n)
                    xr = perm_s[pl.ds(hh * tm + r, n, stride=dil), :]
                    o_ref[hh, rows, :] = epilogue(xr, hh, kind, rows)

    for (kind, dil), ranges in plan:
        cond = None
        for lo, hi in ranges:
            c = (j >= lo) & (j < hi)
            cond = c if cond is None else cond | c
        pl.when(cond)(functools.partial(run, kind, dil))


def _head_plan():
    kinds = []
    for sec in range(3):
        for g in range(N_DIL_GROUPS):
            kinds.append(("rope" if sec < 2 else "cast", DILS[g]))
    kinds.append(("silu", 1))
    kinds += [("norm", 1)] * 4
    kinds += [("cast", 1)] * 2
    kinds += [("silu", 1)] * 2
    plan, lo = {}, 0
    for idx in range(1, len(kinds) + 1):
        if idx == len(kinds) or kinds[idx] != kinds[lo]:
            plan.setdefault(kinds[lo], []).append((lo, idx))
            lo = idx
    return tuple((k, tuple(v)) for k, v in plan.items())


def _project_heads(h, w, b, gain, cos_p, sin_p):
    tm, tn = PROJ_TM, PROJ_TN
    n = w.shape[1]
    per_b = SEQ // tm
    hpb = tn // HEAD_DIM
    return pl.pallas_call(
        functools.partial(_head_proj_kernel, plan=_head_plan()),
        grid=(N_TOK // tm, n // tn),
        in_specs=[pl.BlockSpec((tm, D_MODEL), lambda i, j: (i, 0)),
                  pl.BlockSpec((D_MODEL, tn), lambda i, j: (0, j)),
                  pl.BlockSpec((1, tn), lambda i, j: (0, j)),
                  pl.BlockSpec((1, tn), lambda i, j: (0, j)),
                  pl.BlockSpec((None, tm, HEAD_DIM), lambda i, j: (j % N_DIL_GROUPS, i % per_b, 0)),
                  pl.BlockSpec((None, tm, HEAD_DIM), lambda i, j: (j % N_DIL_GROUPS, i % per_b, 0))],
        out_specs=pl.BlockSpec((None, hpb, tm, HEAD_DIM), lambda i, j: (i // per_b, j, i % per_b, 0)),
        out_shape=jax.ShapeDtypeStruct((BATCH, n // HEAD_DIM, SEQ, HEAD_DIM), BF16),
        scratch_shapes=[pltpu.VMEM((hpb * tm, HEAD_DIM), F32)],
        compiler_params=_params(("arbitrary", "arbitrary")),
        name="proj_heads",
    )(h, w, b.reshape(1, n), gain.reshape(1, n), cos_p, sin_p)


def _gate_proj_kernel(h_ref, w_ref, b_ref, o_ref):
    z = jnp.dot(h_ref[...], w_ref[...], preferred_element_type=F32) + b_ref[...]
    o_ref[...] = _sigmoid(z).astype(BF16)


def _project_gates(h, w, b):
    tm, tn = PROJ_TM, PROJ_TN
    n = w.shape[1]
    return pl.pallas_call(
        _gate_proj_kernel,
        grid=(N_TOK // tm, n // tn),
        in_specs=[pl.BlockSpec((tm, D_MODEL), lambda i, j: (i, 0)),
                  pl.BlockSpec((D_MODEL, tn), lambda i, j: (0, j)),
                  pl.BlockSpec((1, tn), lambda i, j: (0, j))],
        out_specs=pl.BlockSpec((tm, tn), lambda i, j: (i, j)),
        out_shape=jax.ShapeDtypeStruct((N_TOK, n), BF16),
        compiler_params=_params(("arbitrary", "arbitrary")),
        name="proj_gates",
    )(h, w, b.reshape(1, n))


def _forget_kernel(h_ref, w_ref, b_ref, f_ref, carry_ref, *, tc):
    @pl.when(pl.program_id(1) == 0)
    def _():
        carry_ref[...] = jnp.zeros_like(carry_ref)

    z = jnp.dot(h_ref[...], w_ref[...], preferred_element_type=F32) + b_ref[...]
    x = jnp.minimum(z, 0.0) - jnp.log1p(jnp.exp(-jnp.abs(z)))
    row = lax.broadcasted_iota(jnp.int32, x.shape, 0)
    shift = 1
    while shift < tc:
        x = x + jnp.where(row >= shift, pltpu.roll(x, shift, 0), 0.0)
        shift *= 2
    x = x + carry_ref[...]
    f_ref[...] = x
    carry_ref[...] = x[tc - 1:tc, :]


def _forget_cumsum(h, w_f, b_f):
    tc = 512
    per_b = SEQ // tc
    return pl.pallas_call(
        functools.partial(_forget_kernel, tc=tc),
        grid=(BATCH, per_b),
        in_specs=[pl.BlockSpec((tc, D_MODEL), lambda b, c: (b * per_b + c, 0)),
                  pl.BlockSpec((D_MODEL, LANES), lambda b, c: (0, 0)),
                  pl.BlockSpec((1, LANES), lambda b, c: (0, 0))],
        out_specs=pl.BlockSpec((tc, LANES), lambda b, c: (b * per_b + c, 0)),
        out_shape=jax.ShapeDtypeStruct((N_TOK, LANES), F32),
        scratch_shapes=[pltpu.VMEM((1, LANES), F32)],
        compiler_params=_params(("arbitrary", "arbitrary")),
        name="forget_cumsum",
    )(h, w_f, b_f)


def _perm_rows(dil, r, l0, n):
    per_tile = PROJ_TM // dil
    pieces = []
    while n > 0:
        c, off = divmod(l0, per_tile)
        take = min(n, per_tile - off)
        pieces.append((c * PROJ_TM + r * per_tile + off, take))
        l0 += take
        n -= take
    return pieces


def _load_rows(ref, pieces):
    parts = [ref[s:s + n, :] for s, n in pieces]
    return parts[0] if len(parts) == 1 else jnp.concatenate(parts, axis=0)


def _attn_a_kernel(q0, k0, v0, q1, k1, v1, q2, k2, v2, gate_ref, ya_ref, va0, va1, va2, o_s, l_s):
    qi = lax.broadcasted_iota(jnp.int32, (SPAN, 2 * SPAN), 0)
    kj = lax.broadcasted_iota(jnp.int32, (SPAN, 2 * SPAN), 1)
    dist = qi - kj + SPAN
    band = (dist >= 0) & (dist <= SPAN)
    tri = (lax.broadcasted_iota(jnp.int32, (SPAN, SPAN), 0)
           >= lax.broadcasted_iota(jnp.int32, (SPAN, SPAN), 1))

    for g, (q_ref, k_ref, v_ref, vaug_s) in enumerate(((q0, k0, v0, va0), (q1, k1, v1, va1), (q2, k2, v2, va2))):
        dil = DILS[g]
        nb = SEQ // dil // SPAN
        vaug_s[:, HEAD_DIM:] = jnp.ones((SEQ, HEAD_DIM), BF16)
        vaug_s[:, :HEAD_DIM] = v_ref[...]
        for r in range(dil):
            for blk in range(nb):
                cur = _perm_rows(dil, r, blk * SPAN, SPAN)
                keys = (_perm_rows(dil, r, (blk - 1) * SPAN, SPAN) if blk > 0 else []) + cur
                qb = _load_rows(q_ref, cur)
                kb = _load_rows(k_ref, keys)
                vb = _load_rows(vaug_s, keys)
                t = lax.dot_general(qb, kb, (((1,), (1,)), ((), ())), preferred_element_type=F32) * C1
                t = jnp.where(band if blk > 0 else tri, t, -jnp.inf)
                m = jnp.max(t, axis=-1, keepdims=True)
                p = jnp.exp2(t - m)
                pv = jnp.dot(p.astype(BF16), vb, preferred_element_type=F32)
                den = pv[:, HEAD_DIM:]
                o = pv[:, :HEAD_DIM] / den
                lse2 = m + jnp.log2(den)
                if dil == 1:
                    rows = pl.ds(g * SEQ + blk * SPAN, SPAN)
                else:
                    rows = pl.ds(g * SEQ + blk * SPAN * dil + r, SPAN, stride=dil)
                o_s[rows, :] = o
                l_s[rows, :] = lse2

    ch = 256
    for c in range(SEQ // ch):
        ls = [l_s[g * SEQ + c * ch:g * SEQ + (c + 1) * ch, :] for g in range(N_DIL_GROUPS)]
        os_ = [o_s[g * SEQ + c * ch:g * SEQ + (c + 1) * ch, :] for g in range(N_DIL_GROUPS)]
        m = jnp.maximum(jnp.maximum(ls[0], ls[1]), ls[2])
        es = [jnp.exp2(x - m) for x in ls]
        inv = 1.0 / (es[0] + es[1] + es[2])
        ya = (es[0] * inv) * os_[0] + (es[1] * inv) * os_[1] + (es[2] * inv) * os_[2]
        ya_ref[c * ch:(c + 1) * ch, :] = (ya * gate_ref[c * ch:(c + 1) * ch, :].astype(F32)).astype(BF16)


def _attn_a(zh):
    hb = A_WIDTH // HEAD_DIM

    def spec(sec, g):
        base = (sec * N_DIL_GROUPS + g) * hb
        return pl.BlockSpec((None, None, SEQ, HEAD_DIM), lambda b, h: (b, base + h, 0, 0))

    in_specs = [spec(sec, g) for g in range(N_DIL_GROUPS) for sec in range(3)]
    in_specs.append(pl.BlockSpec((None, None, SEQ, HEAD_DIM), lambda b, h: (b, A_G0 // HEAD_DIM + h, 0, 0)))
    return pl.pallas_call(
        _attn_a_kernel,
        grid=(BATCH, A_HEADS),
        in_specs=in_specs,
        out_specs=pl.BlockSpec((None, SEQ, HEAD_DIM), lambda b, h: (b, 0, h)),
        out_shape=jax.ShapeDtypeStruct((BATCH, SEQ, A_WIDTH), BF16),
        scratch_shapes=[pltpu.VMEM((SEQ, 2 * HEAD_DIM), BF16)] * N_DIL_GROUPS + [
                        pltpu.VMEM((N_DIL_GROUPS * SEQ, HEAD_DIM), F32),
                        pltpu.VMEM((N_DIL_GROUPS * SEQ, HEAD_DIM), F32)],
        compiler_params=_params(("arbitrary", "arbitrary")),
        name="attn_a",
    )(*([zh] * 10))


def _fox_kernel(q_ref, k_ref, v_ref, gate_ref, f_ref, y_ref, vaug_s, *, t):
    nq = SEQ // t
    vaug_s[:, :HEAD_DIM] = v_ref[...]
    vaug_s[:, HEAD_DIM:] = jnp.ones((SEQ, HEAD_DIM), BF16)
    row = lax.broadcasted_iota(jnp.int32, (t, t), 0)
    col = lax.broadcasted_iota(jnp.int32, (t, t), 1)
    causal = col <= row

    for i in range(nq):
        qb = q_ref[i * t:(i + 1) * t, :]
        f2q = f_ref[:, i * t:(i + 1) * t] * LOG2E
        fq = jnp.transpose(jnp.broadcast_to(f2q, (LANES, t)))[:, :1]
        m = acc = None
        for j in range(i + 1):
            kb = k_ref[j * t:(j + 1) * t, :]
            f2k = f_ref[:, j * t:(j + 1) * t] * LOG2E
            s = lax.dot_general(qb, kb, (((1,), (1,)), ((), ())), preferred_element_type=F32) * C1 - f2k
            if j == i:
                s = jnp.where(causal, s, -jnp.inf)
            rm = jnp.max(s, axis=-1, keepdims=True) + fq
            m_new = rm if m is None else jnp.maximum(m, rm)
            p = jnp.exp2(s + (fq - m_new))
            pv = jnp.dot(p.astype(BF16), vaug_s[j * t:(j + 1) * t, :], preferred_element_type=F32)
            acc = pv if acc is None else jnp.exp2(m - m_new) * acc + pv
            m = m_new
        o = acc[:, :HEAD_DIM] / acc[:, HEAD_DIM:]
        y_ref[i * t:(i + 1) * t, :] = (o * gate_ref[i * t:(i + 1) * t, :].astype(F32)).astype(BF16)


def _fox_attention(zh, f_row):
    q0, k0, v0, g0 = (c // HEAD_DIM for c in (B_Q0, B_K0, B_V0, B_G0))
    spec = lambda c0: pl.BlockSpec((None, None, SEQ, HEAD_DIM), lambda b, h: (b, c0 + h, 0, 0))
    return pl.pallas_call(
        functools.partial(_fox_kernel, t=256),
        grid=(BATCH, B_HEADS),
        in_specs=[spec(q0), spec(k0), spec(v0), spec(g0),
                  pl.BlockSpec((None, None, 1, SEQ), lambda b, h: (b, h, 0, 0))],
        out_specs=pl.BlockSpec((None, SEQ, HEAD_DIM), lambda b, h: (b, 0, h)),
        out_shape=jax.ShapeDtypeStruct((BATCH, SEQ, B_WIDTH), BF16),
        scratch_shapes=[pltpu.VMEM((SEQ, 2 * HEAD_DIM), BF16)],
        compiler_params=_params(("arbitrary", "arbitrary")),
        name="fox_attention",
    )(zh, zh, zh, zh, f_row)


def _merge_kernel(ya_ref, yb_ref, wa_ref, wb_ref, ga_ref, gb_ref, o_ref):
    pa = jnp.dot(ya_ref[...], wa_ref[...], preferred_element_type=F32)
    pb = jnp.dot(yb_ref[...], wb_ref[...], preferred_element_type=F32)
    o_ref[...] = (ga_ref[...].astype(F32) * pa + gb_ref[...].astype(F32) * pb).astype(BF16)


def _merge(ya, yb, wa, wb, gates):
    tm, tn = 1024, 1024
    nb = D_MODEL // tn
    return pl.pallas_call(
        _merge_kernel,
        grid=(N_TOK // tm, nb),
        in_specs=[pl.BlockSpec((tm, A_WIDTH), lambda i, j: (i, 0)),
                  pl.BlockSpec((tm, B_WIDTH), lambda i, j: (i, 0)),
                  pl.BlockSpec((A_WIDTH, tn), lambda i, j: (0, j)),
                  pl.BlockSpec((B_WIDTH, tn), lambda i, j: (0, j)),
                  pl.BlockSpec((tm, tn), lambda i, j: (i, j)),
                  pl.BlockSpec((tm, tn), lambda i, j: (i, nb + j))],
        out_specs=pl.BlockSpec((tm, tn), lambda i, j: (i, j)),
        out_shape=jax.ShapeDtypeStruct((N_TOK, D_MODEL), BF16),
        compiler_params=_params(("arbitrary", "arbitrary")),
        name="merge",
    )(ya, yb, wa, wb, gates, gates)


def _out_kernel(m_ref, w_ref, x_ref, gate_ref, o_ref):
    out = jnp.dot(m_ref[...], w_ref[...], preferred_element_type=F32)
    o_ref[...] = x_ref[...] + gate_ref[...] * out


def _out_proj(merged, w_o, x2, gate):
    tm, tn = 1024, 512
    per_b = SEQ // tm
    return pl.pallas_call(
        _out_kernel,
        grid=(N_TOK // tm, D_MODEL // tn),
        in_specs=[pl.BlockSpec((tm, D_MODEL), lambda i, j: (i, 0)),
                  pl.BlockSpec((D_MODEL, tn), lambda i, j: (0, j)),
                  pl.BlockSpec((tm, tn), lambda i, j: (i, j)),
                  pl.BlockSpec((None, 1, tn), lambda i, j: (i // per_b, 0, j))],
        out_specs=pl.BlockSpec((tm, tn), lambda i, j: (i, j)),
        out_shape=jax.ShapeDtypeStruct((N_TOK, D_MODEL), F32),
        compiler_params=_params(("arbitrary", "arbitrary")),
        name="out_proj",
    )(merged, w_o, x2, gate.reshape(BATCH, 1, D_MODEL))


def _rope_tables():
    half = HEAD_DIM // 2
    inv = ROPE_THETA ** (-jnp.arange(half, dtype=F32) / half)
    ang = jnp.arange(SEQ).astype(F32)[:, None] * inv[None, :]
    cos, sin = jnp.cos(ang), jnp.sin(ang)
    cos = jnp.concatenate([cos, cos], axis=-1)
    sin = jnp.concatenate([-sin, sin], axis=-1)

    def perm(t, dil):
        t = t.reshape(SEQ // PROJ_TM, PROJ_TM // dil, dil, HEAD_DIM)
        return jnp.transpose(t, (0, 2, 1, 3)).reshape(SEQ, HEAD_DIM)

    return (jnp.stack([perm(cos, d) for d in DILS]), jnp.stack([perm(sin, d) for d in DILS]))


def _head_gains(aq, ak, bq, bk):
    ones = jnp.ones((HEAD_DIM,), F32)
    per_head = ([aq[g] for g in range(N_DIL_GROUPS) for _ in range(A_HEADS)]
                + [ak[g] for g in range(N_DIL_GROUPS) for _ in range(A_HEADS)]
                + [ones] * ((N_DIL_GROUPS + 1) * A_HEADS)
                + [bq] * B_HEADS + [bk] * B_HEADS + [ones] * (2 * B_HEADS))
    return jnp.concatenate(per_head)


def kernel(x, c, norm_g, w_ada, b_ada, w_in, b_in, a_q_norm, a_k_norm, b_q_norm, b_k_norm,
           w_a_out, w_b_out, w_o):
    depth = norm_g.shape[0]
    cos_p, sin_p = _rope_tables()
    xs = x.reshape(N_TOK, D_MODEL)
    for l in range(depth):
        mod = _modulation(c, w_ada[l], b_ada[l])
        shift, scale, gate = mod[:, :D_MODEL], mod[:, D_MODEL:2 * D_MODEL], mod[:, 2 * D_MODEL:]
        h = _norm_modulate(xs, norm_g[l], scale, shift)

        w_bf = w_in[l].astype(BF16)
        gains = _head_gains(a_q_norm[l], a_k_norm[l], b_q_norm[l], b_k_norm[l])
        zh = _project_heads(h, w_bf[:, :B_F0], b_in[l, :B_F0], gains, cos_p, sin_p)
        gates = _project_gates(h, w_bf[:, M_G0:], b_in[l, M_G0:])
        w_f = jnp.pad(w_bf[:, B_F0:M_G0], ((0, 0), (0, LANES - B_HEADS)))
        b_f = jnp.pad(b_in[l, B_F0:M_G0], (0, LANES - B_HEADS)).reshape(1, LANES)
        f_cum = _forget_cumsum(h, w_f, b_f)

        ya = _attn_a(zh)
        f_row = jnp.transpose(f_cum.reshape(BATCH, SEQ, LANES)[:, :, :B_HEADS], (0, 2, 1))[:, :, None, :]
        yb = _fox_attention(zh, f_row)

        merged = _merge(ya.reshape(N_TOK, A_WIDTH), yb.reshape(N_TOK, B_WIDTH),
                        w_a_out[l].astype(BF16), w_b_out[l].astype(BF16), gates)
        xs = _out_proj(merged, w_o[l].astype(BF16), xs, gate)
    return xs.reshape(BATCH, SEQ, D_MODEL)
```

```python
import functools
import math

import jax
import jax.numpy as jnp
import numpy as np
from jax import lax
from jax.experimental import pallas as pl
from jax.experimental.pallas import tpu as pltpu

F32 = jnp.float32
BF16 = jnp.bfloat16

D_MODEL = 4096
BATCH = 4
SEQ = 4096
HEAD_DIM = 128
ROPE_THETA = 10000.0
EPS = 1e-6
DIL_PATTERNS = ((128, 1), (512, 4), (2048, 16))
DILS = tuple(d for _, d in DIL_PATTERNS)
N_DIL_GROUPS = 3
A_HEADS = 8
A_WIDTH = A_HEADS * HEAD_DIM
A_QKV_WIDTH = N_DIL_GROUPS * A_WIDTH
B_HEADS = 16
B_WIDTH = B_HEADS * HEAD_DIM
A_Q0 = 0
A_K0 = A_Q0 + A_QKV_WIDTH
A_V0 = A_K0 + A_QKV_WIDTH
A_G0 = A_V0 + A_QKV_WIDTH
B_Q0 = A_G0 + A_WIDTH
B_K0 = B_Q0 + B_WIDTH
B_V0 = B_K0 + B_WIDTH
B_G0 = B_V0 + B_WIDTH
B_F0 = B_G0 + B_WIDTH
M_G0 = B_F0 + B_HEADS
IN_COLS = M_G0 + 2 * D_MODEL
N_TOK = BATCH * SEQ
SPAN = 128
LANES = 128
VMEM_LIMIT = 56 * 1024 * 1024
PROJ_TM = 1024
PROJ_TN = 1024
HEADS_PER_BLOCK = PROJ_TN // HEAD_DIM
LOG2E = math.log2(math.e)
C1 = (HEAD_DIM ** -0.5) * LOG2E


def _params(sem, vmem=VMEM_LIMIT):
    return pltpu.CompilerParams(dimension_semantics=sem, vmem_limit_bytes=vmem)


def _sigmoid(x):
    return 1.0 / (1.0 + jnp.exp(-x))


def _silu(x):
    return x * _sigmoid(x)


def _mod_kernel(c_ref, w_ref, b_ref, o_ref):
    s = _silu(c_ref[...]).astype(BF16)
    o_ref[...] = jnp.dot(s, w_ref[...].astype(BF16), preferred_element_type=F32) + b_ref[...]


def _modulation(c, w_ada, b_ada):
    tn = 512
    n = w_ada.shape[1]
    cp = jnp.pad(c, ((0, 8 - BATCH), (0, 0)))
    out = pl.pallas_call(
        _mod_kernel,
        grid=(n // tn,),
        in_specs=[pl.BlockSpec((8, D_MODEL), lambda j: (0, 0)),
                  pl.BlockSpec((D_MODEL, tn), lambda j: (0, j)),
                  pl.BlockSpec((1, tn), lambda j: (0, j))],
        out_specs=pl.BlockSpec((8, tn), lambda j: (0, j)),
        out_shape=jax.ShapeDtypeStruct((8, n), F32),
        compiler_params=_params(("arbitrary",)),
        name="modulation",
    )(cp, w_ada, b_ada.reshape(1, n))
    return out[:BATCH]


def _norm_kernel(x_ref, g_ref, scale_ref, shift_ref, h_ref):
    x = x_ref[...]
    ms = jnp.mean(x * x, axis=-1, keepdims=True)
    y = x * lax.rsqrt(ms + EPS) * g_ref[...]
    h_ref[...] = (y * (1.0 + scale_ref[...]) + shift_ref[...]).astype(BF16)


def _norm_modulate(x2, g, scale, shift):
    tm = 512
    per_b = SEQ // tm
    return pl.pallas_call(
        _norm_kernel,
        grid=(N_TOK // tm,),
        in_specs=[pl.BlockSpec((tm, D_MODEL), lambda i: (i, 0)),
                  pl.BlockSpec((1, D_MODEL), lambda i: (0, 0)),
                  pl.BlockSpec((None, 1, D_MODEL), lambda i: (i // per_b, 0, 0)),
                  pl.BlockSpec((None, 1, D_MODEL), lambda i: (i // per_b, 0, 0))],
        out_specs=pl.BlockSpec((tm, D_MODEL), lambda i: (i, 0)),
        out_shape=jax.ShapeDtypeStruct((N_TOK, D_MODEL), BF16),
        compiler_params=_params(("arbitrary",)),
        name="norm_modulate",
    )(x2, g.reshape(1, D_MODEL), scale.reshape(BATCH, 1, D_MODEL), shift.reshape(BATCH, 1, D_MODEL))


def _head_proj_kernel(meta_ref, h_ref, w_ref, b_ref, gain_ref, flag_ref, cos_ref, sin_ref, o_ref, perm_s,
                      *, bodies):
    j = pl.program_id(1)
    tm = h_ref.shape[0]
    chunk = 256

    def epilogue(x, hh, rows, kind):
        lanes = slice(hh * HEAD_DIM, (hh + 1) * HEAD_DIM)
        if kind == "act":
            return jnp.where(flag_ref[:, lanes] > 0.5, _silu(x), x).astype(BF16)
        ms = jnp.mean(x * x, axis=-1, keepdims=True)
        y = x * lax.rsqrt(ms + EPS) * gain_ref[:, lanes]
        if kind == "gen":
            y = jnp.where(flag_ref[:, lanes] > 0.5, y, x)
        y = y * cos_ref[rows, :] + pltpu.roll(y, HEAD_DIM // 2, 1) * sin_ref[rows, :]
        return y.astype(BF16)

    def run(kind, dil):
        acc = jnp.dot(h_ref[...], w_ref[...], preferred_element_type=F32) + b_ref[...]
        for hh in range(HEADS_PER_BLOCK):
            lanes = slice(hh * HEAD_DIM, (hh + 1) * HEAD_DIM)
            if dil == 1:
                for c0 in range(0, tm, chunk):
                    rows = slice(c0, c0 + chunk)
                    o_ref[hh, rows, :] = epilogue(acc[rows, lanes], hh, rows, kind)
            else:
                perm_s[hh * tm:(hh + 1) * tm, :] = acc[:, lanes]
                n = tm // dil
                for r in range(dil):
                    rows = slice(r * n, (r + 1) * n)
                    xr = perm_s[pl.ds(hh * tm + r, n, stride=dil), :]
                    o_ref[hh, rows, :] = epilogue(xr, hh, rows, kind)

    for bid, (kind, dil) in enumerate(bodies):
        pl.when(meta_ref[2, j] == bid)(functools.partial(run, kind, dil))


def _project_heads(h, w, b, gain, flag, cos_t, sin_t, meta, bodies, name):
    tm, tn = PROJ_TM, PROJ_TN
    nj = meta.shape[1]
    per_b = SEQ // tm
    col = lambda i, j, m: (0, m[0, j])
    tab = lambda i, j, m: (m[1, j], i % per_b, 0)
    grid_spec = pltpu.PrefetchScalarGridSpec(
        num_scalar_prefetch=1,
        grid=(N_TOK // tm, nj),
        in_specs=[pl.BlockSpec((tm, D_MODEL), lambda i, j, m: (i, 0)),
                  pl.BlockSpec((D_MODEL, tn), col),
                  pl.BlockSpec((1, tn), col),
                  pl.BlockSpec((1, tn), col),
                  pl.BlockSpec((1, tn), col),
                  pl.BlockSpec((None, tm, HEAD_DIM), tab),
                  pl.BlockSpec((None, tm, HEAD_DIM), tab)],
        out_specs=pl.BlockSpec((None, HEADS_PER_BLOCK, tm, HEAD_DIM),
                               lambda i, j, m: (i // per_b, j, i % per_b, 0)),
        scratch_shapes=[pltpu.VMEM((HEADS_PER_BLOCK * tm, HEAD_DIM), F32)],
    )
    return pl.pallas_call(
        functools.partial(_head_proj_kernel, bodies=bodies),
        grid_spec=grid_spec,
        out_shape=jax.ShapeDtypeStruct((BATCH, nj * HEADS_PER_BLOCK, SEQ, HEAD_DIM), BF16),
        compiler_params=_params(("arbitrary", "arbitrary")),
        name=name,
    )(meta, h, w, b, gain, flag, cos_t, sin_t)


_BLK = {"aq": 0, "ak": 3, "av": 6, "ag": 9, "bq": 10, "bk": 12, "bv": 14, "bg": 16}
_TAB_ID = 3
_NAT_PLAN = ([(_BLK["aq"], 0, 0), (_BLK["ak"], 0, 0)]
             + [(_BLK["bq"] + i, _TAB_ID, 0) for i in range(2)] + [(_BLK["bk"] + i, _TAB_ID, 0) for i in range(2)]
             + [(_BLK["av"], _TAB_ID, 1), (_BLK["ag"], _TAB_ID, 1)]
             + [(_BLK["bv"] + i, _TAB_ID, 1) for i in range(2)] + [(_BLK["bg"] + i, _TAB_ID, 1) for i in range(2)])
_DIL_PLAN = [(_BLK[s] + g, (g if s != "av" else _TAB_ID), g - 1) for g in (1, 2) for s in ("aq", "ak", "av")]
_NAT_HEAD = {"aq": 0, "ak": 8, "bq": 16, "bk": 32, "av": 48, "ag": 56, "bv": 64, "bg": 80}
_DIL_HEAD = {(1, "aq"): 0, (1, "ak"): 8, (1, "av"): 16, (2, "aq"): 24, (2, "ak"): 32, (2, "av"): 40}


def _gate_proj_kernel(h_ref, w_ref, b_ref, o_ref):
    z = jnp.dot(h_ref[...], w_ref[...], preferred_element_type=F32) + b_ref[...]
    o_ref[...] = _sigmoid(z).astype(BF16)


def _project_gates(h, w, b):
    tm, tn = PROJ_TM, PROJ_TN
    n = w.shape[1]
    return pl.pallas_call(
        _gate_proj_kernel,
        grid=(N_TOK // tm, n // tn),
        in_specs=[pl.BlockSpec((tm, D_MODEL), lambda i, j: (i, 0)),
                  pl.BlockSpec((D_MODEL, tn), lambda i, j: (0, j)),
                  pl.BlockSpec((1, tn), lambda i, j: (0, j))],
        out_specs=pl.BlockSpec((tm, tn), lambda i, j: (i, j)),
        out_shape=jax.ShapeDtypeStruct((N_TOK, n), BF16),
        compiler_params=_params(("arbitrary", "arbitrary")),
        name="proj_gates",
    )(h, w, b.reshape(1, n))


def _forget_kernel(h_ref, w_ref, b_ref, f_ref, carry_ref, *, tc):
    @pl.when(pl.program_id(1) == 0)
    def _():
        carry_ref[...] = jnp.zeros_like(carry_ref)

    z = jnp.dot(h_ref[...], w_ref[...], preferred_element_type=F32) + b_ref[...]
    x = jnp.minimum(z, 0.0) - jnp.log1p(jnp.exp(-jnp.abs(z)))
    row = lax.broadcasted_iota(jnp.int32, x.shape, 0)
    shift = 1
    while shift < tc:
        x = x + jnp.where(row >= shift, pltpu.roll(x, shift, 0), 0.0)
        shift *= 2
    x = x + carry_ref[...]
    f_ref[...] = x
    carry_ref[...] = x[tc - 1:tc, :]


def _forget_cumsum(h, w_f, b_f):
    tc = 512
    per_b = SEQ // tc
    return pl.pallas_call(
        functools.partial(_forget_kernel, tc=tc),
        grid=(BATCH, per_b),
        in_specs=[pl.BlockSpec((tc, D_MODEL), lambda b, c: (b * per_b + c, 0)),
                  pl.BlockSpec((D_MODEL, LANES), lambda b, c: (0, 0)),
                  pl.BlockSpec((1, LANES), lambda b, c: (0, 0))],
        out_specs=pl.BlockSpec((tc, LANES), lambda b, c: (b * per_b + c, 0)),
        out_shape=jax.ShapeDtypeStruct((N_TOK, LANES), F32),
        scratch_shapes=[pltpu.VMEM((1, LANES), F32)],
        compiler_params=_params(("arbitrary", "arbitrary")),
        name="forget_cumsum",
    )(h, w_f, b_f)


def _perm_rows(dil, r, l0, n):
    per_tile = PROJ_TM // dil
    pieces = []
    while n > 0:
        c, off = divmod(l0, per_tile)
        take = min(n, per_tile - off)
        pieces.append((c * PROJ_TM + r * per_tile + off, take))
        l0 += take
        n -= take
    return pieces


def _load_rows(ref, pieces):
    parts = [ref[s:s + n, :] for s, n in pieces]
    return parts[0] if len(parts) == 1 else jnp.concatenate(parts, axis=0)


def _attn_a_kernel(q0, k0, v0, q1, k1, v1, q2, k2, v2, gate_ref, ya_ref, va0, va1, va2, o_s, l_s):
    qi = lax.broadcasted_iota(jnp.int32, (SPAN, 2 * SPAN), 0)
    kj = lax.broadcasted_iota(jnp.int32, (SPAN, 2 * SPAN), 1)
    dist = qi - kj + SPAN
    band = (dist >= 0) & (dist <= SPAN)
    tri = (lax.broadcasted_iota(jnp.int32, (SPAN, SPAN), 0)
           >= lax.broadcasted_iota(jnp.int32, (SPAN, SPAN), 1))

    for g, (q_ref, k_ref, v_ref, vaug_s) in enumerate(((q0, k0, v0, va0), (q1, k1, v1, va1), (q2, k2, v2, va2))):
        dil = DILS[g]
        nb = SEQ // dil // SPAN
        vaug_s[:, HEAD_DIM:] = jnp.ones((SEQ, HEAD_DIM), BF16)
        vaug_s[:, :HEAD_DIM] = v_ref[...]
        for r in range(dil):
            for blk in range(nb):
                cur = _perm_rows(dil, r, blk * SPAN, SPAN)
                keys = (_perm_rows(dil, r, (blk - 1) * SPAN, SPAN) if blk > 0 else []) + cur
                qb = _load_rows(q_ref, cur)
                kb = _load_rows(k_ref, keys)
                vb = _load_rows(vaug_s, keys)
                t = lax.dot_general(qb, kb, (((1,), (1,)), ((), ())), preferred_element_type=F32) * C1
                t = jnp.where(band if blk > 0 else tri, t, -jnp.inf)
                m = jnp.max(t, axis=-1, keepdims=True)
                p = jnp.exp2(t - m)
                pv = jnp.dot(p.astype(BF16), vb, preferred_element_type=F32)
                den = pv[:, HEAD_DIM:]
                o = pv[:, :HEAD_DIM] / den
                lse2 = m + jnp.log2(den)
                if dil == 1:
                    rows = pl.ds(g * SEQ + blk * SPAN, SPAN)
                else:
                    rows = pl.ds(g * SEQ + blk * SPAN * dil + r, SPAN, stride=dil)
                o_s[rows, :] = o
                l_s[rows, :] = lse2

    ch = 256
    for c in range(SEQ // ch):
        ls = [l_s[g * SEQ + c * ch:g * SEQ + (c + 1) * ch, :] for g in range(N_DIL_GROUPS)]
        os_ = [o_s[g * SEQ + c * ch:g * SEQ + (c + 1) * ch, :] for g in range(N_DIL_GROUPS)]
        m = jnp.maximum(jnp.maximum(ls[0], ls[1]), ls[2])
        es = [jnp.exp2(x - m) for x in ls]
        inv = 1.0 / (es[0] + es[1] + es[2])
        ya = (es[0] * inv) * os_[0] + (es[1] * inv) * os_[1] + (es[2] * inv) * os_[2]
        ya_ref[c * ch:(c + 1) * ch, :] = (ya * gate_ref[c * ch:(c + 1) * ch, :].astype(F32)).astype(BF16)


def _attn_a(zn, zd):
    def spec(head0):
        return pl.BlockSpec((None, None, SEQ, HEAD_DIM), lambda b, h: (b, head0 + h, 0, 0))

    in_specs = [spec(_NAT_HEAD[s]) for s in ("aq", "ak", "av")]
    in_specs += [spec(_DIL_HEAD[(g, s)]) for g in (1, 2) for s in ("aq", "ak", "av")]
    in_specs.append(spec(_NAT_HEAD["ag"]))
    return pl.pallas_call(
        _attn_a_kernel,
        grid=(BATCH, A_HEADS),
        in_specs=in_specs,
        out_specs=pl.BlockSpec((None, SEQ, HEAD_DIM), lambda b, h: (b, 0, h)),
        out_shape=jax.ShapeDtypeStruct((BATCH, SEQ, A_WIDTH), BF16),
        scratch_shapes=[pltpu.VMEM((SEQ, 2 * HEAD_DIM), BF16)] * N_DIL_GROUPS + [
                        pltpu.VMEM((N_DIL_GROUPS * SEQ, HEAD_DIM), F32),
                        pltpu.VMEM((N_DIL_GROUPS * SEQ, HEAD_DIM), F32)],
        compiler_params=_params(("arbitrary", "arbitrary")),
        name="attn_a",
    )(zn, zn, zn, zd, zd, zd, zd, zd, zd, zn)


def _fox_kernel(q_ref, k_ref, v_ref, gate_ref, f_ref, y_ref, vaug_s, *, t):
    nq = SEQ // t
    vaug_s[:, :HEAD_DIM] = v_ref[...]
    vaug_s[:, HEAD_DIM:] = jnp.ones((SEQ, HEAD_DIM), BF16)
    row = lax.broadcasted_iota(jnp.int32, (t, t), 0)
    col = lax.broadcasted_iota(jnp.int32, (t, t), 1)
    causal = col <= row

    for i in range(nq):
        qb = q_ref[i * t:(i + 1) * t, :]
        f2q = f_ref[:, i * t:(i + 1) * t] * LOG2E
        fq = jnp.transpose(jnp.broadcast_to(f2q, (LANES, t)))[:, :1]
        m = acc = None
        for j in range(i + 1):
            kb = k_ref[j * t:(j + 1) * t, :]
            f2k = f_ref[:, j * t:(j + 1) * t] * LOG2E
            s = lax.dot_general(qb, kb, (((1,), (1,)), ((), ())), preferred_element_type=F32) * C1 - f2k
            if j == i:
                s = jnp.where(causal, s, -jnp.inf)
            rm = jnp.max(s, axis=-1, keepdims=True) + fq
            m_new = rm if m is None else jnp.maximum(m, rm)
            p = jnp.exp2(s + (fq - m_new))
            pv = jnp.dot(p.astype(BF16), vaug_s[j * t:(j + 1) * t, :], preferred_element_type=F32)
            acc = pv if acc is None else jnp.exp2(m - m_new) * acc + pv
            m = m_new
        o = acc[:, :HEAD_DIM] / acc[:, HEAD_DIM:]
        y_ref[i * t:(i + 1) * t, :] = (o * gate_ref[i * t:(i + 1) * t, :].astype(F32)).astype(BF16)


def _fox_attention(zn, f_row):
    spec = lambda s: pl.BlockSpec((None, None, SEQ, HEAD_DIM), lambda b, h: (b, _NAT_HEAD[s] + h, 0, 0))
    return pl.pallas_call(
        functools.partial(_fox_kernel, t=256),
        grid=(BATCH, B_HEADS),
        in_specs=[spec("bq"), spec("bk"), spec("bv"), spec("bg"),
                  pl.BlockSpec((None, None, 1, SEQ), lambda b, h: (b, h, 0, 0))],
        out_specs=pl.BlockSpec((None, SEQ, HEAD_DIM), lambda b, h: (b, 0, h)),
        out_shape=jax.ShapeDtypeStruct((BATCH, SEQ, B_WIDTH), BF16),
        scratch_shapes=[pltpu.VMEM((SEQ, 2 * HEAD_DIM), BF16)],
        compiler_params=_params(("arbitrary", "arbitrary")),
        name="fox_attention",
    )(zn, zn, zn, zn, f_row)


def _merge_kernel(ya_ref, yb_ref, wa_ref, wb_ref, ga_ref, gb_ref, o_ref):
    pa = jnp.dot(ya_ref[...], wa_ref[...], preferred_element_type=F32)
    pb = jnp.dot(yb_ref[...], wb_ref[...], preferred_element_type=F32)
    o_ref[...] = (ga_ref[...].astype(F32) * pa + gb_ref[...].astype(F32) * pb).astype(BF16)


def _merge(ya, yb, wa, wb, gates):
    tm, tn = 1024, 1024
    nb = D_MODEL // tn
    return pl.pallas_call(
        _merge_kernel,
        grid=(N_TOK // tm, nb),
        in_specs=[pl.BlockSpec((tm, A_WIDTH), lambda i, j: (i, 0)),
                  pl.BlockSpec((tm, B_WIDTH), lambda i, j: (i, 0)),
                  pl.BlockSpec((A_WIDTH, tn), lambda i, j: (0, j)),
                  pl.BlockSpec((B_WIDTH, tn), lambda i, j: (0, j)),
                  pl.BlockSpec((tm, tn), lambda i, j: (i, j)),
                  pl.BlockSpec((tm, tn), lambda i, j: (i, nb + j))],
        out_specs=pl.BlockSpec((tm, tn), lambda i, j: (i, j)),
        out_shape=jax.ShapeDtypeStruct((N_TOK, D_MODEL), BF16),
        compiler_params=_params(("arbitrary", "arbitrary")),
        name="merge",
    )(ya, yb, wa, wb, gates, gates)


def _out_kernel(m_ref, w_ref, x_ref, gate_ref, o_ref):
    out = jnp.dot(m_ref[...], w_ref[...], preferred_element_type=F32)
    o_ref[...] = x_ref[...] + gate_ref[...] * out


def _out_proj(merged, w_o, x2, gate):
    tm, tn = 1024, 512
    per_b = SEQ // tm
    return pl.pallas_call(
        _out_kernel,
        grid=(N_TOK // tm, D_MODEL // tn),
        in_specs=[pl.BlockSpec((tm, D_MODEL), lambda i, j: (i, 0)),
                  pl.BlockSpec((D_MODEL, tn), lambda i, j: (0, j)),
                  pl.BlockSpec((tm, tn), lambda i, j: (i, j)),
                  pl.BlockSpec((None, 1, tn), lambda i, j: (i // per_b, 0, j))],
        out_specs=pl.BlockSpec((tm, tn), lambda i, j: (i, j)),
        out_shape=jax.ShapeDtypeStruct((N_TOK, D_MODEL), F32),
        compiler_params=_params(("arbitrary", "arbitrary")),
        name="out_proj",
    )(merged, w_o, x2, gate.reshape(BATCH, 1, D_MODEL))


def _rope_tables():
    half = HEAD_DIM // 2
    inv = ROPE_THETA ** (-jnp.arange(half, dtype=F32) / half)
    ang = jnp.arange(SEQ).astype(F32)[:, None] * inv[None, :]
    cos, sin = jnp.cos(ang), jnp.sin(ang)
    cos = jnp.concatenate([cos, cos], axis=-1)
    sin = jnp.concatenate([-sin, sin], axis=-1)

    def perm(t, dil):
        t = t.reshape(SEQ // PROJ_TM, PROJ_TM // dil, dil, HEAD_DIM)
        return jnp.transpose(t, (0, 2, 1, 3)).reshape(SEQ, HEAD_DIM)

    return (jnp.stack([perm(cos, d) for d in DILS] + [jnp.ones_like(cos)]),
            jnp.stack([perm(sin, d) for d in DILS] + [jnp.zeros_like(sin)]))


def _head_rows(aq, ak, bq, bk):
    ones = jnp.ones((HEAD_DIM,), F32)
    gains = ([aq[g] for g in range(N_DIL_GROUPS) for _ in range(A_HEADS)]
             + [ak[g] for g in range(N_DIL_GROUPS) for _ in range(A_HEADS)]
             + [ones] * ((N_DIL_GROUPS + 1) * A_HEADS)
             + [bq] * B_HEADS + [bk] * B_HEADS + [ones] * (2 * B_HEADS))
    flag = np.zeros((B_F0,), np.float32)
    flag[A_Q0:A_V0] = 1.0
    flag[A_G0:B_Q0] = 1.0
    flag[B_G0:B_F0] = 1.0
    return jnp.concatenate(gains).reshape(1, B_F0), jnp.asarray(flag).reshape(1, B_F0)


def kernel(x, c, norm_g, w_ada, b_ada, w_in, b_in, a_q_norm, a_k_norm, b_q_norm, b_k_norm,
           w_a_out, w_b_out, w_o):
    depth = norm_g.shape[0]
    cos_t, sin_t = _rope_tables()
    nat_meta = jnp.asarray(np.array(_NAT_PLAN, np.int32).T)
    dil_meta = jnp.asarray(np.array(_DIL_PLAN, np.int32).T)
    xs = x.reshape(N_TOK, D_MODEL)
    for l in range(depth):
        mod = _modulation(c, w_ada[l], b_ada[l])
        shift, scale, gate = mod[:, :D_MODEL], mod[:, D_MODEL:2 * D_MODEL], mod[:, 2 * D_MODEL:]
        h = _norm_modulate(xs, norm_g[l], scale, shift)

        w_bf = w_in[l].astype(BF16)
        w_heads, b_heads = w_bf[:, :B_F0], b_in[l, :B_F0].reshape(1, B_F0)
        gains, flags = _head_rows(a_q_norm[l], a_k_norm[l], b_q_norm[l], b_k_norm[l])
        zn = _project_heads(h, w_heads, b_heads, gains, flags, cos_t, sin_t, nat_meta,
                            (("qk", 1), ("act", 1)), "proj_heads_nat")
        zd = _project_heads(h, w_heads, b_heads, gains, flags, cos_t, sin_t, dil_meta,
                            (("gen", DILS[1]), ("gen", DILS[2])), "proj_heads_dil")
        gates = _project_gates(h, w_bf[:, M_G0:], b_in[l, M_G0:])
        w_f = jnp.pad(w_bf[:, B_F0:M_G0], ((0, 0), (0, LANES - B_HEADS)))
        b_f = jnp.pad(b_in[l, B_F0:M_G0], (0, LANES - B_HEADS)).reshape(1, LANES)
        f_cum = _forget_cumsum(h, w_f, b_f)

        ya = _attn_a(zn, zd)
        f_row = jnp.transpose(f_cum.reshape(BATCH, SEQ, LANES)[:, :, :B_HEADS], (0, 2, 1))[:, :, None, :]
        yb = _fox_attention(zn, f_row)

        merged = _merge(ya.reshape(N_TOK, A_WIDTH), yb.reshape(N_TOK, B_WIDTH),
                        w_a_out[l].astype(BF16), w_b_out[l].astype(BF16), gates)
        xs = _out_proj(merged, w_o[l].astype(BF16), xs, gate)
    return xs.reshape(BATCH, SEQ, D_MODEL)
```

```python
import functools
import math

import jax
import jax.numpy as jnp
import numpy as np
from jax import lax
from jax.experimental import pallas as pl
from jax.experimental.pallas import tpu as pltpu

F32 = jnp.float32
BF16 = jnp.bfloat16

D_MODEL = 4096
BATCH = 4
SEQ = 4096
HEAD_DIM = 128
ROPE_THETA = 10000.0
EPS = 1e-6
DIL_PATTERNS = ((128, 1), (512, 4), (2048, 16))
DILS = tuple(d for _, d in DIL_PATTERNS)
N_DIL_GROUPS = 3
A_HEADS = 8
A_WIDTH = A_HEADS * HEAD_DIM
A_QKV_WIDTH = N_DIL_GROUPS * A_WIDTH
B_HEADS = 16
B_WIDTH = B_HEADS * HEAD_DIM
A_Q0 = 0
A_K0 = A_Q0 + A_QKV_WIDTH
A_V0 = A_K0 + A_QKV_WIDTH
A_G0 = A_V0 + A_QKV_WIDTH
B_Q0 = A_G0 + A_WIDTH
B_K0 = B_Q0 + B_WIDTH
B_V0 = B_K0 + B_WIDTH
B_G0 = B_V0 + B_WIDTH
B_F0 = B_G0 + B_WIDTH
M_G0 = B_F0 + B_HEADS
IN_COLS = M_G0 + 2 * D_MODEL
N_TOK = BATCH * SEQ
SPAN = 128
LANES = 128
VMEM_LIMIT = 56 * 1024 * 1024
PROJ_TM = 1024
PROJ_TN = 1024
HEADS_PER_BLOCK = PROJ_TN // HEAD_DIM
LOG2E = math.log2(math.e)
C1 = (HEAD_DIM ** -0.5) * LOG2E


def _params(sem, vmem=VMEM_LIMIT):
    return pltpu.CompilerParams(dimension_semantics=sem, vmem_limit_bytes=vmem)


def _sigmoid(x):
    return 1.0 / (1.0 + jnp.exp(-x))


def _silu(x):
    return x * _sigmoid(x)


def _mod_kernel(c_ref, w_ref, b_ref, o_ref):
    s = _silu(c_ref[...]).astype(BF16)
    o_ref[...] = jnp.dot(s, w_ref[...].astype(BF16), preferred_element_type=F32) + b_ref[...]


def _modulation(c, w_ada, b_ada):
    tn = 512
    n = w_ada.shape[1]
    cp = jnp.pad(c, ((0, 8 - BATCH), (0, 0)))
    out = pl.pallas_call(
        _mod_kernel,
        grid=(n // tn,),
        in_specs=[pl.BlockSpec((8, D_MODEL), lambda j: (0, 0)),
                  pl.BlockSpec((D_MODEL, tn), lambda j: (0, j)),
                  pl.BlockSpec((1, tn), lambda j: (0, j))],
        out_specs=pl.BlockSpec((8, tn), lambda j: (0, j)),
        out_shape=jax.ShapeDtypeStruct((8, n), F32),
        compiler_params=_params(("arbitrary",)),
        name="modulation",
    )(cp, w_ada, b_ada.reshape(1, n))
    return out[:BATCH]


def _norm_kernel(x_ref, g_ref, scale_ref, shift_ref, h_ref):
    x = x_ref[...]
    ms = jnp.mean(x * x, axis=-1, keepdims=True)
    y = x * lax.rsqrt(ms + EPS) * g_ref[...]
    h_ref[...] = (y * (1.0 + scale_ref[...]) + shift_ref[...]).astype(BF16)


def _norm_modulate(x2, g, scale, shift):
    tm = 512
    per_b = SEQ // tm
    return pl.pallas_call(
        _norm_kernel,
        grid=(N_TOK // tm,),
        in_specs=[pl.BlockSpec((tm, D_MODEL), lambda i: (i, 0)),
                  pl.BlockSpec((1, D_MODEL), lambda i: (0, 0)),
                  pl.BlockSpec((None, 1, D_MODEL), lambda i: (i // per_b, 0, 0)),
                  pl.BlockSpec((None, 1, D_MODEL), lambda i: (i // per_b, 0, 0))],
        out_specs=pl.BlockSpec((tm, D_MODEL), lambda i: (i, 0)),
        out_shape=jax.ShapeDtypeStruct((N_TOK, D_MODEL), BF16),
        compiler_params=_params(("arbitrary",)),
        name="norm_modulate",
    )(x2, g.reshape(1, D_MODEL), scale.reshape(BATCH, 1, D_MODEL), shift.reshape(BATCH, 1, D_MODEL))


def _head_proj_kernel(meta_ref, h_ref, w_ref, b_ref, gain_ref, flag_ref, cos_ref, sin_ref, o_ref, perm_s,
                      *, bodies):
    j = pl.program_id(1)
    tm = h_ref.shape[0]
    chunk = 256

    def epilogue(x, hh, rows, kind):
        lanes = slice(hh * HEAD_DIM, (hh + 1) * HEAD_DIM)
        if kind == "act":
            return jnp.where(flag_ref[:, lanes] > 0.5, _silu(x), x).astype(BF16)
        ms = jnp.mean(x * x, axis=-1, keepdims=True)
        y = x * lax.rsqrt(ms + EPS) * gain_ref[:, lanes]
        if kind == "gen":
            y = jnp.where(flag_ref[:, lanes] > 0.5, y, x)
        y = y * cos_ref[rows, :] + pltpu.roll(y, HEAD_DIM // 2, 1) * sin_ref[rows, :]
        return y.astype(BF16)

    def run(kind, dil):
        acc = jnp.dot(h_ref[...], w_ref[...], preferred_element_type=F32) + b_ref[...]
        for hh in range(HEADS_PER_BLOCK):
            lanes = slice(hh * HEAD_DIM, (hh + 1) * HEAD_DIM)
            if kind == "act":
                for c0 in range(0, tm, chunk):
                    rows = slice(c0, c0 + chunk)
                    o_ref[hh, rows, :] = epilogue(acc[rows, lanes], hh, rows, kind)
                continue
            perm_s[hh * tm:(hh + 1) * tm, :] = acc[:, lanes]
            if dil == 1:
                for c0 in range(0, tm, chunk):
                    rows = slice(c0, c0 + chunk)
                    start = pl.multiple_of(hh * tm + c0 + meta_ref[3, j], chunk)
                    o_ref[hh, rows, :] = epilogue(perm_s[pl.ds(start, chunk), :], hh, rows, kind)
            else:
                n = tm // dil
                for r in range(dil):
                    rows = slice(r * n, (r + 1) * n)
                    xr = perm_s[pl.ds(hh * tm + r, n, stride=dil), :]
                    o_ref[hh, rows, :] = epilogue(xr, hh, rows, kind)

    for bid, (kind, dil) in enumerate(bodies):
        pl.when(meta_ref[2, j] == bid)(functools.partial(run, kind, dil))


def _project_heads(h, w, b, gain, flag, cos_t, sin_t, meta, bodies, name):
    tm, tn = PROJ_TM, PROJ_TN
    nj = meta.shape[1]
    per_b = SEQ // tm
    col = lambda i, j, m: (0, m[0, j])
    tab = lambda i, j, m: (m[1, j], i % per_b, 0)
    grid_spec = pltpu.PrefetchScalarGridSpec(
        num_scalar_prefetch=1,
        grid=(N_TOK // tm, nj),
        in_specs=[pl.BlockSpec((tm, D_MODEL), lambda i, j, m: (i, 0)),
                  pl.BlockSpec((D_MODEL, tn), col),
                  pl.BlockSpec((1, tn), col),
                  pl.BlockSpec((1, tn), col),
                  pl.BlockSpec((1, tn), col),
                  pl.BlockSpec((None, tm, HEAD_DIM), tab),
                  pl.BlockSpec((None, tm, HEAD_DIM), tab)],
        out_specs=pl.BlockSpec((None, HEADS_PER_BLOCK, tm, HEAD_DIM),
                               lambda i, j, m: (i // per_b, j, i % per_b, 0)),
        scratch_shapes=[pltpu.VMEM((HEADS_PER_BLOCK * tm, HEAD_DIM), F32)],
    )
    return pl.pallas_call(
        functools.partial(_head_proj_kernel, bodies=bodies),
        grid_spec=grid_spec,
        out_shape=jax.ShapeDtypeStruct((BATCH, nj * HEADS_PER_BLOCK, SEQ, HEAD_DIM), BF16),
        compiler_params=_params(("arbitrary", "arbitrary")),
        name=name,
    )(meta, h, w, b, gain, flag, cos_t, sin_t)


_BLK = {"aq": 0, "ak": 3, "av": 6, "ag": 9, "bq": 10, "bk": 12, "bv": 14, "bg": 16}
_TAB_ID = 3
_NAT_PLAN = ([(_BLK["aq"], 0, 0), (_BLK["ak"], 0, 0)]
             + [(_BLK["bq"] + i, _TAB_ID, 0) for i in range(2)] + [(_BLK["bk"] + i, _TAB_ID, 0) for i in range(2)]
             + [(_BLK["av"], _TAB_ID, 1), (_BLK["ag"], _TAB_ID, 1)]
             + [(_BLK["bv"] + i, _TAB_ID, 1) for i in range(2)] + [(_BLK["bg"] + i, _TAB_ID, 1) for i in range(2)])
_DIL_PLAN = [(_BLK[s] + g, (g if s != "av" else _TAB_ID), g - 1) for g in (1, 2) for s in ("aq", "ak", "av")]
_NAT_HEAD = {"aq": 0, "ak": 8, "bq": 16, "bk": 32, "av": 48, "ag": 56, "bv": 64, "bg": 80}
_DIL_HEAD = {(1, "aq"): 0, (1, "ak"): 8, (1, "av"): 16, (2, "aq"): 24, (2, "ak"): 32, (2, "av"): 40}


def _gate_proj_kernel(h_ref, w_ref, b_ref, o_ref):
    z = jnp.dot(h_ref[...], w_ref[...], preferred_element_type=F32) + b_ref[...]
    o_ref[...] = _sigmoid(z).astype(BF16)


def _project_gates(h, w, b):
    tm, tn = PROJ_TM, PROJ_TN
    n = w.shape[1]
    return pl.pallas_call(
        _gate_proj_kernel,
        grid=(N_TOK // tm, n // tn),
        in_specs=[pl.BlockSpec((tm, D_MODEL), lambda i, j: (i, 0)),
                  pl.BlockSpec((D_MODEL, tn), lambda i, j: (0, j)),
                  pl.BlockSpec((1, tn), lambda i, j: (0, j))],
        out_specs=pl.BlockSpec((tm, tn), lambda i, j: (i, j)),
        out_shape=jax.ShapeDtypeStruct((N_TOK, n), BF16),
        compiler_params=_params(("arbitrary", "arbitrary")),
        name="proj_gates",
    )(h, w, b.reshape(1, n))


def _forget_kernel(h_ref, w_ref, b_ref, f_ref, carry_ref, *, tc):
    @pl.when(pl.program_id(1) == 0)
    def _():
        carry_ref[...] = jnp.zeros_like(carry_ref)

    z = jnp.dot(h_ref[...], w_ref[...], preferred_element_type=F32) + b_ref[...]
    x = jnp.minimum(z, 0.0) - jnp.log1p(jnp.exp(-jnp.abs(z)))
    row = lax.broadcasted_iota(jnp.int32, x.shape, 0)
    shift = 1
    while shift < tc:
        x = x + jnp.where(row >= shift, pltpu.roll(x, shift, 0), 0.0)
        shift *= 2
    x = x + carry_ref[...]
    f_ref[...] = x
    carry_ref[...] = x[tc - 1:tc, :]


def _forget_cumsum(h, w_f, b_f):
    tc = 512
    per_b = SEQ // tc
    return pl.pallas_call(
        functools.partial(_forget_kernel, tc=tc),
        grid=(BATCH, per_b),
        in_specs=[pl.BlockSpec((tc, D_MODEL), lambda b, c: (b * per_b + c, 0)),
                  pl.BlockSpec((D_MODEL, LANES), lambda b, c: (0, 0)),
                  pl.BlockSpec((1, LANES), lambda b, c: (0, 0))],
        out_specs=pl.BlockSpec((tc, LANES), lambda b, c: (b * per_b + c, 0)),
        out_shape=jax.ShapeDtypeStruct((N_TOK, LANES), F32),
        scratch_shapes=[pltpu.VMEM((1, LANES), F32)],
        compiler_params=_params(("arbitrary", "arbitrary")),
        name="forget_cumsum",
    )(h, w_f, b_f)


def _perm_rows(dil, r, l0, n):
    per_tile = PROJ_TM // dil
    pieces = []
    while n > 0:
        c, off = divmod(l0, per_tile)
        take = min(n, per_tile - off)
        pieces.append((c * PROJ_TM + r * per_tile + off, take))
        l0 += take
        n -= take
    return pieces


def _load_rows(ref, pieces):
    parts = [ref[s:s + n, :] for s, n in pieces]
    return parts[0] if len(parts) == 1 else jnp.concatenate(parts, axis=0)


def _attn_a_kernel(q0, k0, v0, q1, k1, v1, q2, k2, v2, gate_ref, ya_ref, va0, va1, va2, o_s, l_s):
    qi = lax.broadcasted_iota(jnp.int32, (SPAN, 2 * SPAN), 0)
    kj = lax.broadcasted_iota(jnp.int32, (SPAN, 2 * SPAN), 1)
    dist = qi - kj + SPAN
    band = (dist >= 0) & (dist <= SPAN)
    tri = (lax.broadcasted_iota(jnp.int32, (SPAN, SPAN), 0)
           >= lax.broadcasted_iota(jnp.int32, (SPAN, SPAN), 1))

    for g, (q_ref, k_ref, v_ref, vaug_s) in enumerate(((q0, k0, v0, va0), (q1, k1, v1, va1), (q2, k2, v2, va2))):
        dil = DILS[g]
        nb = SEQ // dil // SPAN
        vaug_s[:, HEAD_DIM:] = jnp.ones((SEQ, HEAD_DIM), BF16)
        vaug_s[:, :HEAD_DIM] = v_ref[...]
        for r in range(dil):
            for blk in range(nb):
                cur = _perm_rows(dil, r, blk * SPAN, SPAN)
                keys = (_perm_rows(dil, r, (blk - 1) * SPAN, SPAN) if blk > 0 else []) + cur
                qb = _load_rows(q_ref, cur)
                kb = _load_rows(k_ref, keys)
                vb = _load_rows(vaug_s, keys)
                t = lax.dot_general(qb, kb, (((1,), (1,)), ((), ())), preferred_element_type=F32) * C1
                t = jnp.where(band if blk > 0 else tri, t, -jnp.inf)
                m = jnp.max(t, axis=-1, keepdims=True)
                p = jnp.exp2(t - m)
                pv = jnp.dot(p.astype(BF16), vb, preferred_element_type=F32)
                den = pv[:, HEAD_DIM:]
                o = pv[:, :HEAD_DIM] / den
                lse2 = m + jnp.log2(den)
                if dil == 1:
                    rows = pl.ds(g * SEQ + blk * SPAN, SPAN)
                else:
                    rows = pl.ds(g * SEQ + blk * SPAN * dil + r, SPAN, stride=dil)
                o_s[rows, :] = o
                l_s[rows, :] = lse2

    ch = 256
    for c in range(SEQ // ch):
        ls = [l_s[g * SEQ + c * ch:g * SEQ + (c + 1) * ch, :] for g in range(N_DIL_GROUPS)]
        os_ = [o_s[g * SEQ + c * ch:g * SEQ + (c + 1) * ch, :] for g in range(N_DIL_GROUPS)]
        m = jnp.maximum(jnp.maximum(ls[0], ls[1]), ls[2])
        es = [jnp.exp2(x - m) for x in ls]
        inv = 1.0 / (es[0] + es[1] + es[2])
        ya = (es[0] * inv) * os_[0] + (es[1] * inv) * os_[1] + (es[2] * inv) * os_[2]
        ya_ref[c * ch:(c + 1) * ch, :] = (ya * gate_ref[c * ch:(c + 1) * ch, :].astype(F32)).astype(BF16)


def _attn_a(zn, zd):
    def spec(head0):
        return pl.BlockSpec((None, None, SEQ, HEAD_DIM), lambda b, h: (b, head0 + h, 0, 0))

    in_specs = [spec(_NAT_HEAD[s]) for s in ("aq", "ak", "av")]
    in_specs += [spec(_DIL_HEAD[(g, s)]) for g in (1, 2) for s in ("aq", "ak", "av")]
    in_specs.append(spec(_NAT_HEAD["ag"]))
    return pl.pallas_call(
        _attn_a_kernel,
        grid=(BATCH, A_HEADS),
        in_specs=in_specs,
        out_specs=pl.BlockSpec((None, SEQ, HEAD_DIM), lambda b, h: (b, 0, h)),
        out_shape=jax.ShapeDtypeStruct((BATCH, SEQ, A_WIDTH), BF16),
        scratch_shapes=[pltpu.VMEM((SEQ, 2 * HEAD_DIM), BF16)] * N_DIL_GROUPS + [
                        pltpu.VMEM((N_DIL_GROUPS * SEQ, HEAD_DIM), F32),
                        pltpu.VMEM((N_DIL_GROUPS * SEQ, HEAD_DIM), F32)],
        compiler_params=_params(("arbitrary", "arbitrary")),
        name="attn_a",
    )(zn, zn, zn, zd, zd, zd, zd, zd, zd, zn)


def _fox_kernel(q_ref, k_ref, v_ref, gate_ref, f_ref, y_ref, vaug_s, s_s, *, t):
    nq = SEQ // t
    vaug_s[:, :HEAD_DIM] = v_ref[...]
    vaug_s[:, HEAD_DIM:] = jnp.ones((SEQ, HEAD_DIM), BF16)
    row = lax.broadcasted_iota(jnp.int32, (t, t), 0)
    col = lax.broadcasted_iota(jnp.int32, (t, t), 1)
    causal = col <= row

    for i in range(nq):
        qb = q_ref[i * t:(i + 1) * t, :]
        f2q = f_ref[:, i * t:(i + 1) * t] * LOG2E
        fq = jnp.transpose(jnp.broadcast_to(f2q, (LANES, t)))[:, :1]
        buf = (i % 2) * SEQ
        m = None
        for j in range(i + 1):
            kb = k_ref[j * t:(j + 1) * t, :]
            f2k = f_ref[:, j * t:(j + 1) * t] * LOG2E
            s = lax.dot_general(qb, kb, (((1,), (1,)), ((), ())), preferred_element_type=F32) * C1 - f2k
            if j == i:
                s = jnp.where(causal, s, -jnp.inf)
            s_s[:, buf + j * t:buf + (j + 1) * t] = s
            rm = jnp.max(s, axis=-1, keepdims=True)
            m = rm if m is None else jnp.maximum(m, rm)
        shift = jnp.broadcast_to((fq - (m + fq)), (t, t))
        acc = None
        for j in range(i + 1):
            p = jnp.exp2(s_s[:, buf + j * t:buf + (j + 1) * t] + shift)
            pv = jnp.dot(p.astype(BF16), vaug_s[j * t:(j + 1) * t, :], preferred_element_type=F32)
            acc = pv if acc is None else acc + pv
        o = acc[:, :HEAD_DIM] / acc[:, HEAD_DIM:]
        y_ref[i * t:(i + 1) * t, :] = (o * gate_ref[i * t:(i + 1) * t, :].astype(F32)).astype(BF16)


def _fox_attention(zn, f_row):
    spec = lambda s: pl.BlockSpec((None, None, SEQ, HEAD_DIM), lambda b, h: (b, _NAT_HEAD[s] + h, 0, 0))
    return pl.pallas_call(
        functools.partial(_fox_kernel, t=256),
        grid=(BATCH, B_HEADS),
        in_specs=[spec("bq"), spec("bk"), spec("bv"), spec("bg"),
                  pl.BlockSpec((None, None, 1, SEQ), lambda b, h: (b, h, 0, 0))],
        out_specs=pl.BlockSpec((None, SEQ, HEAD_DIM), lambda b, h: (b, 0, h)),
        out_shape=jax.ShapeDtypeStruct((BATCH, SEQ, B_WIDTH), BF16),
        scratch_shapes=[pltpu.VMEM((SEQ, 2 * HEAD_DIM), BF16), pltpu.VMEM((256, 2 * SEQ), F32)],
        compiler_params=_params(("arbitrary", "arbitrary")),
        name="fox_attention",
    )(zn, zn, zn, zn, f_row)


def _merge_kernel(ya_ref, yb_ref, wa_ref, wb_ref, ga_ref, gb_ref, o_ref):
    pa = jnp.dot(ya_ref[...], wa_ref[...], preferred_element_type=F32)
    pb = jnp.dot(yb_ref[...], wb_ref[...], preferred_element_type=F32)
    o_ref[...] = (ga_ref[...].astype(F32) * pa + gb_ref[...].astype(F32) * pb).astype(BF16)


def _merge(ya, yb, wa, wb, gates):
    tm, tn = 1024, 1024
    nb = D_MODEL // tn
    return pl.pallas_call(
        _merge_kernel,
        grid=(N_TOK // tm, nb),
        in_specs=[pl.BlockSpec((tm, A_WIDTH), lambda i, j: (i, 0)),
                  pl.BlockSpec((tm, B_WIDTH), lambda i, j: (i, 0)),
                  pl.BlockSpec((A_WIDTH, tn), lambda i, j: (0, j)),
                  pl.BlockSpec((B_WIDTH, tn), lambda i, j: (0, j)),
                  pl.BlockSpec((tm, tn), lambda i, j: (i, j)),
                  pl.BlockSpec((tm, tn), lambda i, j: (i, nb + j))],
        out_specs=pl.BlockSpec((tm, tn), lambda i, j: (i, j)),
        out_shape=jax.ShapeDtypeStruct((N_TOK, D_MODEL), BF16),
        compiler_params=_params(("arbitrary", "arbitrary")),
        name="merge",
    )(ya, yb, wa, wb, gates, gates)


def _out_kernel(m_ref, w_ref, x_ref, gate_ref, o_ref):
    out = jnp.dot(m_ref[...], w_ref[...], preferred_element_type=F32)
    o_ref[...] = x_ref[...] + gate_ref[...] * out


def _out_proj(merged, w_o, x2, gate):
    tm, tn = 1024, 512
    per_b = SEQ // tm
    return pl.pallas_call(
        _out_kernel,
        grid=(N_TOK // tm, D_MODEL // tn),
        in_specs=[pl.BlockSpec((tm, D_MODEL), lambda i, j: (i, 0)),
                  pl.BlockSpec((D_MODEL, tn), lambda i, j: (0, j)),
                  pl.BlockSpec((tm, tn), lambda i, j: (i, j)),
                  pl.BlockSpec((None, 1, tn), lambda i, j: (i // per_b, 0, j))],
        out_specs=pl.BlockSpec((tm, tn), lambda i, j: (i, j)),
        out_shape=jax.ShapeDtypeStruct((N_TOK, D_MODEL), F32),
        compiler_params=_params(("arbitrary", "arbitrary")),
        name="out_proj",
    )(merged, w_o, x2, gate.reshape(BATCH, 1, D_MODEL))


def _rope_tables():
    half = HEAD_DIM // 2
    inv = ROPE_THETA ** (-jnp.arange(half, dtype=F32) / half)
    ang = jnp.arange(SEQ).astype(F32)[:, None] * inv[None, :]
    cos, sin = jnp.cos(ang), jnp.sin(ang)
    cos = jnp.concatenate([cos, cos], axis=-1)
    sin = jnp.concatenate([-sin, sin], axis=-1)

    def perm(t, dil):
        t = t.reshape(SEQ // PROJ_TM, PROJ_TM // dil, dil, HEAD_DIM)
        return jnp.transpose(t, (0, 2, 1, 3)).reshape(SEQ, HEAD_DIM)

    return (jnp.stack([perm(cos, d) for d in DILS] + [jnp.ones_like(cos)]),
            jnp.stack([perm(sin, d) for d in DILS] + [jnp.zeros_like(sin)]))


def _head_rows(aq, ak, bq, bk):
    ones = jnp.ones((HEAD_DIM,), F32)
    gains = ([aq[g] for g in range(N_DIL_GROUPS) for _ in range(A_HEADS)]
             + [ak[g] for g in range(N_DIL_GROUPS) for _ in range(A_HEADS)]
             + [ones] * ((N_DIL_GROUPS + 1) * A_HEADS)
             + [bq] * B_HEADS + [bk] * B_HEADS + [ones] * (2 * B_HEADS))
    flag = np.zeros((B_F0,), np.float32)
    flag[A_Q0:A_V0] = 1.0
    flag[A_G0:B_Q0] = 1.0
    flag[B_G0:B_F0] = 1.0
    return jnp.concatenate(gains).reshape(1, B_F0), jnp.asarray(flag).reshape(1, B_F0)


def kernel(x, c, norm_g, w_ada, b_ada, w_in, b_in, a_q_norm, a_k_norm, b_q_norm, b_k_norm,
           w_a_out, w_b_out, w_o):
    depth = norm_g.shape[0]
    cos_t, sin_t = _rope_tables()
    nat_meta = jnp.asarray(np.array([p + (0,) for p in _NAT_PLAN], np.int32).T)
    dil_meta = jnp.asarray(np.array([p + (0,) for p in _DIL_PLAN], np.int32).T)
    xs = x.reshape(N_TOK, D_MODEL)
    for l in range(depth):
        mod = _modulation(c, w_ada[l], b_ada[l])
        shift, scale, gate = mod[:, :D_MODEL], mod[:, D_MODEL:2 * D_MODEL], mod[:, 2 * D_MODEL:]
        h = _norm_modulate(xs, norm_g[l], scale, shift)

        w_heads, b_heads = w_in[l][:, :B_F0].astype(BF16), b_in[l, :B_F0].reshape(1, B_F0)
        gains, flags = _head_rows(a_q_norm[l], a_k_norm[l], b_q_norm[l], b_k_norm[l])
        zn = _project_heads(h, w_heads, b_heads, gains, flags, cos_t, sin_t, nat_meta,
                            (("qk", 1), ("act", 1)), "proj_heads_nat")
        zd = _project_heads(h, w_heads, b_heads, gains, flags, cos_t, sin_t, dil_meta,
                            (("gen", DILS[1]), ("gen", DILS[2])), "proj_heads_dil")
        gates = _project_gates(h, w_in[l][:, M_G0:].astype(BF16), b_in[l, M_G0:])
        w_f = jnp.pad(w_in[l][:, B_F0:M_G0].astype(BF16), ((0, 0), (0, LANES - B_HEADS)))
        b_f = jnp.pad(b_in[l, B_F0:M_G0], (0, LANES - B_HEADS)).reshape(1, LANES)
        f_cum = _forget_cumsum(h, w_f, b_f)

        ya = _attn_a(zn, zd)
        f_row = jnp.transpose(f_cum.reshape(BATCH, SEQ, LANES)[:, :, :B_HEADS], (0, 2, 1))[:, :, None, :]
        yb = _fox_attention(zn, f_row)

        merged = _merge(ya.reshape(N_TOK, A_WIDTH), yb.reshape(N_TOK, B_WIDTH),
                        w_a_out[l].astype(BF16), w_b_out[l].astype(BF16), gates)
        xs = _out_proj(merged, w_o[l].astype(BF16), xs, gate)
    return xs.reshape(BATCH, SEQ, D_MODEL)
```

```python
import functools
import math

import jax
import jax.numpy as jnp
import numpy as np
from jax import lax
from jax.experimental import pallas as pl
from jax.experimental.pallas import tpu as pltpu

F32 = jnp.float32
BF16 = jnp.bfloat16

D_MODEL = 4096
BATCH = 4
SEQ = 4096
HEAD_DIM = 128
ROPE_THETA = 10000.0
EPS = 1e-6
DIL_PATTERNS = ((128, 1), (512, 4), (2048, 16))
DILS = tuple(d for _, d in DIL_PATTERNS)
N_DIL_GROUPS = 3
A_HEADS = 8
A_WIDTH = A_HEADS * HEAD_DIM
A_QKV_WIDTH = N_DIL_GROUPS * A_WIDTH
B_HEADS = 16
B_WIDTH = B_HEADS * HEAD_DIM
A_Q0 = 0
A_K0 = A_Q0 + A_QKV_WIDTH
A_V0 = A_K0 + A_QKV_WIDTH
A_G0 = A_V0 + A_QKV_WIDTH
B_Q0 = A_G0 + A_WIDTH
B_K0 = B_Q0 + B_WIDTH
B_V0 = B_K0 + B_WIDTH
B_G0 = B_V0 + B_WIDTH
B_F0 = B_G0 + B_WIDTH
M_G0 = B_F0 + B_HEADS
IN_COLS = M_G0 + 2 * D_MODEL
N_TOK = BATCH * SEQ
SPAN = 128
LANES = 128
VMEM_LIMIT = 56 * 1024 * 1024
PROJ_TM = 1024
PROJ_TN = 1024
HEADS_PER_BLOCK = PROJ_TN // HEAD_DIM
LOG2E = math.log2(math.e)
C1 = (HEAD_DIM ** -0.5) * LOG2E


def _params(sem, vmem=VMEM_LIMIT):
    return pltpu.CompilerParams(dimension_semantics=sem, vmem_limit_bytes=vmem)


def _dot_nt(a, bt):
    return lax.dot_general(a, bt, (((1,), (1,)), ((), ())), preferred_element_type=F32)


def _sigmoid(x):
    return 1.0 / (1.0 + jnp.exp(-x))


def _silu(x):
    return x * _sigmoid(x)


def _mod_kernel(c_ref, w_ref, b_ref, o_ref):
    s = _silu(c_ref[...]).astype(BF16)
    o_ref[...] = jnp.dot(s, w_ref[...].astype(BF16), preferred_element_type=F32) + b_ref[...]


def _modulation(c, w_ada, b_ada):
    tn = 512
    n = w_ada.shape[1]
    cp = jnp.pad(c, ((0, 8 - BATCH), (0, 0)))
    out = pl.pallas_call(
        _mod_kernel,
        grid=(n // tn,),
        in_specs=[pl.BlockSpec((8, D_MODEL), lambda j: (0, 0)),
                  pl.BlockSpec((D_MODEL, tn), lambda j: (0, j)),
                  pl.BlockSpec((1, tn), lambda j: (0, j))],
        out_specs=pl.BlockSpec((8, tn), lambda j: (0, j)),
        out_shape=jax.ShapeDtypeStruct((8, n), F32),
        compiler_params=_params(("arbitrary",)),
        name="modulation",
    )(cp, w_ada, b_ada.reshape(1, n))
    return out[:BATCH]


def _norm_kernel(x_ref, g_ref, scale_ref, shift_ref, h_ref):
    x = x_ref[...]
    ms = jnp.mean(x * x, axis=-1, keepdims=True)
    y = x * lax.rsqrt(ms + EPS) * g_ref[...]
    h_ref[...] = (y * (1.0 + scale_ref[...]) + shift_ref[...]).astype(BF16)


def _norm_modulate(x2, g, scale, shift):
    tm = 512
    per_b = SEQ // tm
    return pl.pallas_call(
        _norm_kernel,
        grid=(N_TOK // tm,),
        in_specs=[pl.BlockSpec((tm, D_MODEL), lambda i: (i, 0)),
                  pl.BlockSpec((1, D_MODEL), lambda i: (0, 0)),
                  pl.BlockSpec((None, 1, D_MODEL), lambda i: (i // per_b, 0, 0)),
                  pl.BlockSpec((None, 1, D_MODEL), lambda i: (i // per_b, 0, 0))],
        out_specs=pl.BlockSpec((tm, D_MODEL), lambda i: (i, 0)),
        out_shape=jax.ShapeDtypeStruct((N_TOK, D_MODEL), BF16),
        compiler_params=_params(("arbitrary",)),
        name="norm_modulate",
    )(x2, g.reshape(1, D_MODEL), scale.reshape(BATCH, 1, D_MODEL), shift.reshape(BATCH, 1, D_MODEL))


def _head_proj_kernel(meta_ref, h_ref, w_ref, b_ref, gain_ref, flag_ref, cos_ref, sin_ref, o_ref, perm_s,
                      *, bodies):
    j = pl.program_id(1)
    tm = h_ref.shape[0]
    chunk = 256

    def epilogue(x, hh, rows, kind):
        lanes = slice(hh * HEAD_DIM, (hh + 1) * HEAD_DIM)
        if kind == "act":
            return jnp.where(flag_ref[:, lanes] > 0.5, _silu(x), x).astype(BF16)
        ms = jnp.mean(x * x, axis=-1, keepdims=True)
        y = x * lax.rsqrt(ms + EPS) * gain_ref[:, lanes]
        if kind == "gen":
            y = jnp.where(flag_ref[:, lanes] > 0.5, y, x)
        y = y * cos_ref[rows, :] + pltpu.roll(y, HEAD_DIM // 2, 1) * sin_ref[rows, :]
        return y.astype(BF16)

    def run(kind, dil):
        acc = _dot_nt(h_ref[...], w_ref[...]) + b_ref[...]
        for hh in range(HEADS_PER_BLOCK):
            lanes = slice(hh * HEAD_DIM, (hh + 1) * HEAD_DIM)
            if kind == "act":
                for c0 in range(0, tm, chunk):
                    rows = slice(c0, c0 + chunk)
                    o_ref[hh, rows, :] = epilogue(acc[rows, lanes], hh, rows, kind)
                continue
            perm_s[hh * tm:(hh + 1) * tm, :] = acc[:, lanes]
            if dil == 1:
                for c0 in range(0, tm, chunk):
                    rows = slice(c0, c0 + chunk)
                    start = pl.multiple_of(hh * tm + c0 + meta_ref[3, j], chunk)
                    o_ref[hh, rows, :] = epilogue(perm_s[pl.ds(start, chunk), :], hh, rows, kind)
            else:
                n = tm // dil
                for r in range(dil):
                    rows = slice(r * n, (r + 1) * n)
                    xr = perm_s[pl.ds(hh * tm + r, n, stride=dil), :]
                    o_ref[hh, rows, :] = epilogue(xr, hh, rows, kind)

    for bid, (kind, dil) in enumerate(bodies):
        pl.when(meta_ref[2, j] == bid)(functools.partial(run, kind, dil))


def _project_heads(h, w, b, gain, flag, cos_t, sin_t, meta, bodies, name):
    tm, tn = PROJ_TM, PROJ_TN
    nj = meta.shape[1]
    per_b = SEQ // tm
    col = lambda i, j, m: (0, m[0, j])
    wrow = lambda i, j, m: (m[0, j], 0)
    tab = lambda i, j, m: (m[1, j], i % per_b, 0)
    grid_spec = pltpu.PrefetchScalarGridSpec(
        num_scalar_prefetch=1,
        grid=(N_TOK // tm, nj),
        in_specs=[pl.BlockSpec((tm, D_MODEL), lambda i, j, m: (i, 0)),
                  pl.BlockSpec((tn, D_MODEL), wrow),
                  pl.BlockSpec((1, tn), col),
                  pl.BlockSpec((1, tn), col),
                  pl.BlockSpec((1, tn), col),
                  pl.BlockSpec((None, tm, HEAD_DIM), tab),
                  pl.BlockSpec((None, tm, HEAD_DIM), tab)],
        out_specs=pl.BlockSpec((None, HEADS_PER_BLOCK, tm, HEAD_DIM),
                               lambda i, j, m: (i // per_b, j, i % per_b, 0)),
        scratch_shapes=[pltpu.VMEM((HEADS_PER_BLOCK * tm, HEAD_DIM), F32)],
    )
    return pl.pallas_call(
        functools.partial(_head_proj_kernel, bodies=bodies),
        grid_spec=grid_spec,
        out_shape=jax.ShapeDtypeStruct((BATCH, nj * HEADS_PER_BLOCK, SEQ, HEAD_DIM), BF16),
        compiler_params=_params(("arbitrary", "arbitrary")),
        name=name,
    )(meta, h, w, b, gain, flag, cos_t, sin_t)


_BLK = {"aq": 0, "ak": 3, "av": 6, "ag": 9, "bq": 10, "bk": 12, "bv": 14, "bg": 16}
_TAB_ID = 3
_NAT_PLAN = ([(_BLK["aq"], 0, 0), (_BLK["ak"], 0, 0)]
             + [(_BLK["bq"] + i, _TAB_ID, 0) for i in range(2)] + [(_BLK["bk"] + i, _TAB_ID, 0) for i in range(2)]
             + [(_BLK["av"], _TAB_ID, 1), (_BLK["ag"], _TAB_ID, 1)]
             + [(_BLK["bv"] + i, _TAB_ID, 1) for i in range(2)] + [(_BLK["bg"] + i, _TAB_ID, 1) for i in range(2)])
_DIL_PLAN = [(_BLK[s] + g, (g if s != "av" else _TAB_ID), g - 1) for g in (1, 2) for s in ("aq", "ak", "av")]
_NAT_HEAD = {"aq": 0, "ak": 8, "bq": 16, "bk": 32, "av": 48, "ag": 56, "bv": 64, "bg": 80}
_DIL_HEAD = {(1, "aq"): 0, (1, "ak"): 8, (1, "av"): 16, (2, "aq"): 24, (2, "ak"): 32, (2, "av"): 40}


def _gate_proj_kernel(h_ref, w_ref, b_ref, o_ref):
    z = _dot_nt(h_ref[...], w_ref[...]) + b_ref[...]
    o_ref[...] = _sigmoid(z).astype(BF16)


def _project_gates(h, w, b):
    tm, tn = PROJ_TM, PROJ_TN
    n = w.shape[0]
    return pl.pallas_call(
        _gate_proj_kernel,
        grid=(N_TOK // tm, n // tn),
        in_specs=[pl.BlockSpec((tm, D_MODEL), lambda i, j: (i, 0)),
                  pl.BlockSpec((tn, D_MODEL), lambda i, j: (j, 0)),
                  pl.BlockSpec((1, tn), lambda i, j: (0, j))],
        out_specs=pl.BlockSpec((tm, tn), lambda i, j: (i, j)),
        out_shape=jax.ShapeDtypeStruct((N_TOK, n), BF16),
        compiler_params=_params(("arbitrary", "arbitrary")),
        name="proj_gates",
    )(h, w, b.reshape(1, n))


def _forget_kernel(h_ref, w_ref, b_ref, f_ref, carry_ref, *, tc):
    @pl.when(pl.program_id(1) == 0)
    def _():
        carry_ref[...] = jnp.zeros_like(carry_ref)

    z = _dot_nt(h_ref[...], w_ref[...]) + b_ref[...]
    x = jnp.minimum(z, 0.0) - jnp.log1p(jnp.exp(-jnp.abs(z)))
    row = lax.broadcasted_iota(jnp.int32, x.shape, 0)
    shift = 1
    while shift < tc:
        x = x + jnp.where(row >= shift, pltpu.roll(x, shift, 0), 0.0)
        shift *= 2
    x = x + carry_ref[...]
    f_ref[...] = x
    carry_ref[...] = x[tc - 1:tc, :]


def _forget_cumsum(h, w_f, b_f):
    tc = 512
    per_b = SEQ // tc
    return pl.pallas_call(
        functools.partial(_forget_kernel, tc=tc),
        grid=(BATCH, per_b),
        in_specs=[pl.BlockSpec((tc, D_MODEL), lambda b, c: (b * per_b + c, 0)),
                  pl.BlockSpec((LANES, D_MODEL), lambda b, c: (0, 0)),
                  pl.BlockSpec((1, LANES), lambda b, c: (0, 0))],
        out_specs=pl.BlockSpec((tc, LANES), lambda b, c: (b * per_b + c, 0)),
        out_shape=jax.ShapeDtypeStruct((N_TOK, LANES), F32),
        scratch_shapes=[pltpu.VMEM((1, LANES), F32)],
        compiler_params=_params(("arbitrary", "arbitrary")),
        name="forget_cumsum",
    )(h, w_f, b_f)


def _perm_rows(dil, r, l0, n):
    per_tile = PROJ_TM // dil
    pieces = []
    while n > 0:
        c, off = divmod(l0, per_tile)
        take = min(n, per_tile - off)
        pieces.append((c * PROJ_TM + r * per_tile + off, take))
        l0 += take
        n -= take
    return pieces


def _load_rows(ref, pieces):
    parts = [ref[s:s + n, :] for s, n in pieces]
    return parts[0] if len(parts) == 1 else jnp.concatenate(parts, axis=0)


def _attn_a_kernel(q0, k0, v0, q1, k1, v1, q2, k2, v2, gate_ref, ya_ref, va0, va1, va2, o_s, l_s):
    qi = lax.broadcasted_iota(jnp.int32, (SPAN, 2 * SPAN), 0)
    kj = lax.broadcasted_iota(jnp.int32, (SPAN, 2 * SPAN), 1)
    dist = qi - kj + SPAN
    band = (dist >= 0) & (dist <= SPAN)
    tri = (lax.broadcasted_iota(jnp.int32, (SPAN, SPAN), 0)
           >= lax.broadcasted_iota(jnp.int32, (SPAN, SPAN), 1))

    for g, (q_ref, k_ref, v_ref, vaug_s) in enumerate(((q0, k0, v0, va0), (q1, k1, v1, va1), (q2, k2, v2, va2))):
        dil = DILS[g]
        nb = SEQ // dil // SPAN
        vaug_s[:, HEAD_DIM:] = jnp.ones((SEQ, HEAD_DIM), BF16)
        vaug_s[:, :HEAD_DIM] = v_ref[...]
        for r in range(dil):
            for blk in range(nb):
                cur = _perm_rows(dil, r, blk * SPAN, SPAN)
                keys = (_perm_rows(dil, r, (blk - 1) * SPAN, SPAN) if blk > 0 else []) + cur
                qb = _load_rows(q_ref, cur)
                kb = _load_rows(k_ref, keys)
                vb = _load_rows(vaug_s, keys)
                t = lax.dot_general(qb, kb, (((1,), (1,)), ((), ())), preferred_element_type=F32) * C1
                t = jnp.where(band if blk > 0 else tri, t, -jnp.inf)
                m = jnp.max(t, axis=-1, keepdims=True)
                p = jnp.exp2(t - m)
                pv = jnp.dot(p.astype(BF16), vb, preferred_element_type=F32)
                den = pv[:, HEAD_DIM:]
                o = pv[:, :HEAD_DIM] / den
                lse2 = m + jnp.log2(den)
                if dil == 1:
                    rows = pl.ds(g * SEQ + blk * SPAN, SPAN)
                else:
                    rows = pl.ds(g * SEQ + blk * SPAN * dil + r, SPAN, stride=dil)
                o_s[rows, :] = o
                l_s[rows, :] = lse2

    ch = 256
    for c in range(SEQ // ch):
        ls = [l_s[g * SEQ + c * ch:g * SEQ + (c + 1) * ch, :] for g in range(N_DIL_GROUPS)]
        os_ = [o_s[g * SEQ + c * ch:g * SEQ + (c + 1) * ch, :] for g in range(N_DIL_GROUPS)]
        m = jnp.maximum(jnp.maximum(ls[0], ls[1]), ls[2])
        es = [jnp.exp2(x - m) for x in ls]
        inv = 1.0 / (es[0] + es[1] + es[2])
        ya = (es[0] * inv) * os_[0] + (es[1] * inv) * os_[1] + (es[2] * inv) * os_[2]
        ya_ref[c * ch:(c + 1) * ch, :] = (ya * gate_ref[c * ch:(c + 1) * ch, :].astype(F32)).astype(BF16)


def _attn_a(zn, zd):
    def spec(head0):
        return pl.BlockSpec((None, None, SEQ, HEAD_DIM), lambda b, h: (b, head0 + h, 0, 0))

    in_specs = [spec(_NAT_HEAD[s]) for s in ("aq", "ak", "av")]
    in_specs += [spec(_DIL_HEAD[(g, s)]) for g in (1, 2) for s in ("aq", "ak", "av")]
    in_specs.append(spec(_NAT_HEAD["ag"]))
    return pl.pallas_call(
        _attn_a_kernel,
        grid=(BATCH, A_HEADS),
        in_specs=in_specs,
        out_specs=pl.BlockSpec((None, SEQ, HEAD_DIM), lambda b, h: (b, 0, h)),
        out_shape=jax.ShapeDtypeStruct((BATCH, SEQ, A_WIDTH), BF16),
        scratch_shapes=[pltpu.VMEM((SEQ, 2 * HEAD_DIM), BF16)] * N_DIL_GROUPS + [
                        pltpu.VMEM((N_DIL_GROUPS * SEQ, HEAD_DIM), F32),
                        pltpu.VMEM((N_DIL_GROUPS * SEQ, HEAD_DIM), F32)],
        compiler_params=_params(("arbitrary", "arbitrary")),
        name="attn_a",
    )(zn, zn, zn, zd, zd, zd, zd, zd, zd, zn)


def _fox_kernel(q_ref, k_ref, v_ref, gate_ref, f_ref, y_ref, vaug_s, s_s, *, t):
    nq = SEQ // t
    vaug_s[:, :HEAD_DIM] = v_ref[...]
    vaug_s[:, HEAD_DIM:] = jnp.ones((SEQ, HEAD_DIM), BF16)
    kw = 2 * t

    shifts = {}

    def chunks(i):
        n_keys = (i + 1) * t
        return [(c0, min(kw, n_keys - c0)) for c0 in range(0, n_keys, kw)]

    def first_pass(i):
        qb = q_ref[i * t:(i + 1) * t, :]
        f2q = f_ref[:, i * t:(i + 1) * t] * LOG2E
        fq = jnp.transpose(jnp.broadcast_to(f2q, (LANES, t)))[:, :1]
        buf = (i % 2) * SEQ
        m = None
        for c0, w in chunks(i):
            kb = k_ref[c0:c0 + w, :]
            f2k = f_ref[:, c0:c0 + w] * LOG2E
            s = lax.dot_general(qb, kb, (((1,), (1,)), ((), ())), preferred_element_type=F32) * C1 - f2k
            if c0 + w == (i + 1) * t:
                row = lax.broadcasted_iota(jnp.int32, (t, w), 0) + i * t
                col = lax.broadcasted_iota(jnp.int32, (t, w), 1) + c0
                s = jnp.where(col <= row, s, -jnp.inf)
            s_s[:, buf + c0:buf + c0 + w] = s
            rm = jnp.max(s, axis=-1, keepdims=True)
            m = rm if m is None else jnp.maximum(m, rm)
            yield
        shifts[i] = fq - (m + fq)

    def second_pass(i):
        buf = (i % 2) * SEQ
        acc = None
        for c0, w in chunks(i):
            p = jnp.exp2(s_s[:, buf + c0:buf + c0 + w] + shifts[i])
            pv = jnp.dot(p.astype(BF16), vaug_s[c0:c0 + w, :], preferred_element_type=F32)
            acc = pv if acc is None else acc + pv
            yield
        o = acc[:, :HEAD_DIM] / acc[:, HEAD_DIM:]
        y_ref[i * t:(i + 1) * t, :] = (o * gate_ref[i * t:(i + 1) * t, :].astype(F32)).astype(BF16)

    for _ in first_pass(0):
        pass
    for i in range(nq):
        live = [second_pass(i)] + ([first_pass(i + 1)] if i + 1 < nq else [])
        while live:
            live = [g for g in live if next(g, StopIteration) is not StopIteration]


def _fox_attention(zn, f_row):
    spec = lambda s: pl.BlockSpec((None, None, SEQ, HEAD_DIM), lambda b, h: (b, _NAT_HEAD[s] + h, 0, 0))
    return pl.pallas_call(
        functools.partial(_fox_kernel, t=256),
        grid=(BATCH, B_HEADS),
        in_specs=[spec("bq"), spec("bk"), spec("bv"), spec("bg"),
                  pl.BlockSpec((None, None, 1, SEQ), lambda b, h: (b, h, 0, 0))],
        out_specs=pl.BlockSpec((None, SEQ, HEAD_DIM), lambda b, h: (b, 0, h)),
        out_shape=jax.ShapeDtypeStruct((BATCH, SEQ, B_WIDTH), BF16),
        scratch_shapes=[pltpu.VMEM((SEQ, 2 * HEAD_DIM), BF16), pltpu.VMEM((256, 2 * SEQ), F32)],
        compiler_params=_params(("arbitrary", "arbitrary")),
        name="fox_attention",
    )(zn, zn, zn, zn, f_row)


def _merge_kernel(ya_ref, yb_ref, wa_ref, wb_ref, ga_ref, gb_ref, o_ref):
    pa = jnp.dot(ya_ref[...], wa_ref[...], preferred_element_type=F32)
    pb = jnp.dot(yb_ref[...], wb_ref[...], preferred_element_type=F32)
    o_ref[...] = (ga_ref[...].astype(F32) * pa + gb_ref[...].astype(F32) * pb).astype(BF16)


def _merge(ya, yb, wa, wb, gates):
    tm, tn = 1024, 1024
    nb = D_MODEL // tn
    return pl.pallas_call(
        _merge_kernel,
        grid=(N_TOK // tm, nb),
        in_specs=[pl.BlockSpec((tm, A_WIDTH), lambda i, j: (i, 0)),
                  pl.BlockSpec((tm, B_WIDTH), lambda i, j: (i, 0)),
                  pl.BlockSpec((A_WIDTH, tn), lambda i, j: (0, j)),
                  pl.BlockSpec((B_WIDTH, tn), lambda i, j: (0, j)),
                  pl.BlockSpec((tm, tn), lambda i, j: (i, j)),
                  pl.BlockSpec((tm, tn), lambda i, j: (i, nb + j))],
        out_specs=pl.BlockSpec((tm, tn), lambda i, j: (i, j)),
        out_shape=jax.ShapeDtypeStruct((N_TOK, D_MODEL), BF16),
        compiler_params=_params(("arbitrary", "arbitrary")),
        name="merge",
    )(ya, yb, wa, wb, gates, gates)


def _out_kernel(m_ref, w_ref, x_ref, gate_ref, o_ref):
    out = jnp.dot(m_ref[...], w_ref[...], preferred_element_type=F32)
    o_ref[...] = x_ref[...] + gate_ref[...] * out


def _out_proj(merged, w_o, x2, gate):
    tm, tn = 1024, 512
    per_b = SEQ // tm
    return pl.pallas_call(
        _out_kernel,
        grid=(N_TOK // tm, D_MODEL // tn),
        in_specs=[pl.BlockSpec((tm, D_MODEL), lambda i, j: (i, 0)),
                  pl.BlockSpec((D_MODEL, tn), lambda i, j: (0, j)),
                  pl.BlockSpec((tm, tn), lambda i, j: (i, j)),
                  pl.BlockSpec((None, 1, tn), lambda i, j: (i // per_b, 0, j))],
        out_specs=pl.BlockSpec((tm, tn), lambda i, j: (i, j)),
        out_shape=jax.ShapeDtypeStruct((N_TOK, D_MODEL), F32),
        compiler_params=_params(("arbitrary", "arbitrary")),
        name="out_proj",
    )(merged, w_o, x2, gate.reshape(BATCH, 1, D_MODEL))


def _rope_tables():
    half = HEAD_DIM // 2
    inv = ROPE_THETA ** (-jnp.arange(half, dtype=F32) / half)
    ang = jnp.arange(SEQ).astype(F32)[:, None] * inv[None, :]
    cos, sin = jnp.cos(ang), jnp.sin(ang)
    cos = jnp.concatenate([cos, cos], axis=-1)
    sin = jnp.concatenate([-sin, sin], axis=-1)

    def perm(t, dil):
        t = t.reshape(SEQ // PROJ_TM, PROJ_TM // dil, dil, HEAD_DIM)
        return jnp.transpose(t, (0, 2, 1, 3)).reshape(SEQ, HEAD_DIM)

    return (jnp.stack([perm(cos, d) for d in DILS] + [jnp.ones_like(cos)]),
            jnp.stack([perm(sin, d) for d in DILS] + [jnp.zeros_like(sin)]))


def _head_rows(aq, ak, bq, bk):
    ones = jnp.ones((HEAD_DIM,), F32)
    gains = ([aq[g] for g in range(N_DIL_GROUPS) for _ in range(A_HEADS)]
             + [ak[g] for g in range(N_DIL_GROUPS) for _ in range(A_HEADS)]
             + [ones] * ((N_DIL_GROUPS + 1) * A_HEADS)
             + [bq] * B_HEADS + [bk] * B_HEADS + [ones] * (2 * B_HEADS))
    flag = np.zeros((B_F0,), np.float32)
    flag[A_Q0:A_V0] = 1.0
    flag[A_G0:B_Q0] = 1.0
    flag[B_G0:B_F0] = 1.0
    return jnp.concatenate(gains).reshape(1, B_F0), jnp.asarray(flag).reshape(1, B_F0)


def kernel(x, c, norm_g, w_ada, b_ada, w_in, b_in, a_q_norm, a_k_norm, b_q_norm, b_k_norm,
           w_a_out, w_b_out, w_o):
    depth = norm_g.shape[0]
    cos_t, sin_t = _rope_tables()
    nat_meta = jnp.asarray(np.array([p + (0,) for p in _NAT_PLAN], np.int32).T)
    dil_meta = jnp.asarray(np.array([p + (0,) for p in _DIL_PLAN], np.int32).T)
    xs = x.reshape(N_TOK, D_MODEL)
    for l in range(depth):
        mod = _modulation(c, w_ada[l], b_ada[l])
        shift, scale, gate = mod[:, :D_MODEL], mod[:, D_MODEL:2 * D_MODEL], mod[:, 2 * D_MODEL:]
        h = _norm_modulate(xs, norm_g[l], scale, shift)

        w_t = jnp.swapaxes(w_in[l], 0, 1)
        w_heads, b_heads = w_t[:B_F0].astype(BF16), b_in[l, :B_F0].reshape(1, B_F0)
        gains, flags = _head_rows(a_q_norm[l], a_k_norm[l], b_q_norm[l], b_k_norm[l])
        zn = _project_heads(h, w_heads, b_heads, gains, flags, cos_t, sin_t, nat_meta,
                            (("qk", 1), ("act", 1)), "proj_heads_nat")
        zd = _project_heads(h, w_heads, b_heads, gains, flags, cos_t, sin_t, dil_meta,
                            (("gen", DILS[1]), ("gen", DILS[2])), "proj_heads_dil")
        gates = _project_gates(h, w_t[M_G0:].astype(BF16), b_in[l, M_G0:])
        w_f = jnp.pad(w_t[B_F0:M_G0].astype(BF16), ((0, LANES - B_HEADS), (0, 0)))
        b_f = jnp.pad(b_in[l, B_F0:M_G0], (0, LANES - B_HEADS)).reshape(1, LANES)
        f_cum = _forget_cumsum(h, w_f, b_f)

        ya = _attn_a(zn, zd)
        f_row = jnp.transpose(f_cum.reshape(BATCH, SEQ, LANES)[:, :, :B_HEADS], (0, 2, 1))[:, :, None, :]
        yb = _fox_attention(zn, f_row)

        merged = _merge(ya.reshape(N_TOK, A_WIDTH), yb.reshape(N_TOK, B_WIDTH),
                        w_a_out[l].astype(BF16), w_b_out[l].astype(BF16), gates)
        xs = _out_proj(merged, w_o[l].astype(BF16), xs, gate)
    return xs.reshape(BATCH, SEQ, D_MODEL)
```

```python
import functools
import math

import jax
import jax.numpy as jnp
import numpy as np
from jax import lax
from jax.experimental import pallas as pl
from jax.experimental.pallas import tpu as pltpu

F32 = jnp.float32
BF16 = jnp.bfloat16

D_MODEL = 4096
BATCH = 4
SEQ = 4096
HEAD_DIM = 128
ROPE_THETA = 10000.0
EPS = 1e-6
DIL_PATTERNS = ((128, 1), (512, 4), (2048, 16))
DILS = tuple(d for _, d in DIL_PATTERNS)
N_DIL_GROUPS = 3
A_HEADS = 8
A_WIDTH = A_HEADS * HEAD_DIM
A_QKV_WIDTH = N_DIL_GROUPS * A_WIDTH
B_HEADS = 16
B_WIDTH = B_HEADS * HEAD_DIM
A_Q0 = 0
A_K0 = A_Q0 + A_QKV_WIDTH
A_V0 = A_K0 + A_QKV_WIDTH
A_G0 = A_V0 + A_QKV_WIDTH
B_Q0 = A_G0 + A_WIDTH
B_K0 = B_Q0 + B_WIDTH
B_V0 = B_K0 + B_WIDTH
B_G0 = B_V0 + B_WIDTH
B_F0 = B_G0 + B_WIDTH
M_G0 = B_F0 + B_HEADS
IN_COLS = M_G0 + 2 * D_MODEL
N_TOK = BATCH * SEQ
SPAN = 128
LANES = 128
VMEM_LIMIT = 56 * 1024 * 1024
PROJ_TM = 1024
PROJ_TN = 1024
HEADS_PER_BLOCK = PROJ_TN // HEAD_DIM
LOG2E = math.log2(math.e)
C1 = (HEAD_DIM ** -0.5) * LOG2E


def _params(sem, vmem=VMEM_LIMIT):
    return pltpu.CompilerParams(dimension_semantics=sem, vmem_limit_bytes=vmem)


def _dot_nt(a, bt):
    return lax.dot_general(a, bt, (((1,), (1,)), ((), ())), preferred_element_type=F32)


def _sigmoid(x):
    return 1.0 / (1.0 + jnp.exp(-x))


def _silu(x):
    return x * _sigmoid(x)


def _mod_kernel(c_ref, w_ref, b_ref, o_ref):
    s = _silu(c_ref[...]).astype(BF16)
    o_ref[...] = jnp.dot(s, w_ref[...].astype(BF16), preferred_element_type=F32) + b_ref[...]


def _modulation(c, w_ada, b_ada):
    tn = 512
    n = w_ada.shape[1]
    cp = jnp.pad(c, ((0, 8 - BATCH), (0, 0)))
    out = pl.pallas_call(
        _mod_kernel,
        grid=(n // tn,),
        in_specs=[pl.BlockSpec((8, D_MODEL), lambda j: (0, 0)),
                  pl.BlockSpec((D_MODEL, tn), lambda j: (0, j)),
                  pl.BlockSpec((1, tn), lambda j: (0, j))],
        out_specs=pl.BlockSpec((8, tn), lambda j: (0, j)),
        out_shape=jax.ShapeDtypeStruct((8, n), F32),
        compiler_params=_params(("arbitrary",)),
        name="modulation",
    )(cp, w_ada, b_ada.reshape(1, n))
    return out[:BATCH]


def _norm_kernel(x_ref, g_ref, scale_ref, shift_ref, h_ref):
    x = x_ref[...]
    ms = jnp.mean(x * x, axis=-1, keepdims=True)
    y = x * lax.rsqrt(ms + EPS) * g_ref[...]
    h_ref[...] = (y * (1.0 + scale_ref[...]) + shift_ref[...]).astype(BF16)


def _norm_modulate(x2, g, scale, shift):
    tm = 512
    per_b = SEQ // tm
    return pl.pallas_call(
        _norm_kernel,
        grid=(N_TOK // tm,),
        in_specs=[pl.BlockSpec((tm, D_MODEL), lambda i: (i, 0)),
                  pl.BlockSpec((1, D_MODEL), lambda i: (0, 0)),
                  pl.BlockSpec((None, 1, D_MODEL), lambda i: (i // per_b, 0, 0)),
                  pl.BlockSpec((None, 1, D_MODEL), lambda i: (i // per_b, 0, 0))],
        out_specs=pl.BlockSpec((tm, D_MODEL), lambda i: (i, 0)),
        out_shape=jax.ShapeDtypeStruct((N_TOK, D_MODEL), BF16),
        compiler_params=_params(("arbitrary",)),
        name="norm_modulate",
    )(x2, g.reshape(1, D_MODEL), scale.reshape(BATCH, 1, D_MODEL), shift.reshape(BATCH, 1, D_MODEL))


def _head_proj_kernel(meta_ref, h_ref, w_ref, b_ref, gain_ref, flag_ref, cos_ref, sin_ref, o_ref, perm_s,
                      *, bodies):
    j = pl.program_id(1)
    tm = h_ref.shape[0]
    chunk = 256

    def epilogue(x, hh, rows, kind):
        lanes = slice(hh * HEAD_DIM, (hh + 1) * HEAD_DIM)
        if kind == "act":
            return jnp.where(flag_ref[:, lanes] > 0.5, _silu(x), x).astype(BF16)
        ms = jnp.mean(x * x, axis=-1, keepdims=True)
        y = x * lax.rsqrt(ms + EPS) * gain_ref[:, lanes]
        if kind == "gen":
            y = jnp.where(flag_ref[:, lanes] > 0.5, y, x)
        y = y * cos_ref[rows, :] + pltpu.roll(y, HEAD_DIM // 2, 1) * sin_ref[rows, :]
        return y.astype(BF16)

    def run(kind, dil):
        acc = _dot_nt(h_ref[...], w_ref[...]) + b_ref[...]
        for hh in range(HEADS_PER_BLOCK):
            lanes = slice(hh * HEAD_DIM, (hh + 1) * HEAD_DIM)
            if kind == "act":
                for c0 in range(0, tm, chunk):
                    rows = slice(c0, c0 + chunk)
                    o_ref[hh, rows, :] = epilogue(acc[rows, lanes], hh, rows, kind)
                continue
            perm_s[hh * tm:(hh + 1) * tm, :] = acc[:, lanes]
            if dil == 1:
                for c0 in range(0, tm, chunk):
                    rows = slice(c0, c0 + chunk)
                    start = pl.multiple_of(hh * tm + c0 + meta_ref[3, j], chunk)
                    o_ref[hh, rows, :] = epilogue(perm_s[pl.ds(start, chunk), :], hh, rows, kind)
            else:
                n = tm // dil
                for r in range(dil):
                    rows = slice(r * n, (r + 1) * n)
                    xr = perm_s[pl.ds(hh * tm + r, n, stride=dil), :]
                    o_ref[hh, rows, :] = epilogue(xr, hh, rows, kind)

    for bid, (kind, dil) in enumerate(bodies):
        pl.when(meta_ref[2, j] == bid)(functools.partial(run, kind, dil))


def _project_heads(h, w, b, gain, flag, cos_t, sin_t, meta, bodies, name):
    tm, tn = PROJ_TM, PROJ_TN
    nj = meta.shape[1]
    per_b = SEQ // tm
    col = lambda i, j, m: (0, m[0, j])
    wrow = lambda i, j, m: (m[0, j], 0)
    tab = lambda i, j, m: (m[1, j], i % per_b, 0)
    grid_spec = pltpu.PrefetchScalarGridSpec(
        num_scalar_prefetch=1,
        grid=(N_TOK // tm, nj),
        in_specs=[pl.BlockSpec((tm, D_MODEL), lambda i, j, m: (i, 0)),
                  pl.BlockSpec((tn, D_MODEL), wrow),
                  pl.BlockSpec((1, tn), col),
                  pl.BlockSpec((1, tn), col),
                  pl.BlockSpec((1, tn), col),
                  pl.BlockSpec((None, tm, HEAD_DIM), tab),
                  pl.BlockSpec((None, tm, HEAD_DIM), tab)],
        out_specs=pl.BlockSpec((None, HEADS_PER_BLOCK, tm, HEAD_DIM),
                               lambda i, j, m: (i // per_b, j, i % per_b, 0)),
        scratch_shapes=[pltpu.VMEM((HEADS_PER_BLOCK * tm, HEAD_DIM), F32)],
    )
    return pl.pallas_call(
        functools.partial(_head_proj_kernel, bodies=bodies),
        grid_spec=grid_spec,
        out_shape=jax.ShapeDtypeStruct((BATCH, nj * HEADS_PER_BLOCK, SEQ, HEAD_DIM), BF16),
        compiler_params=_params(("arbitrary", "arbitrary")),
        name=name,
    )(meta, h, w, b, gain, flag, cos_t, sin_t)


_BLK = {"aq": 0, "ak": 3, "av": 6, "ag": 9, "bq": 10, "bk": 12, "bv": 14, "bg": 16}
_TAB_ID = 3
_NAT_PLAN = ([(_BLK["aq"], 0, 0), (_BLK["ak"], 0, 0)]
             + [(_BLK["bq"] + i, _TAB_ID, 0) for i in range(2)] + [(_BLK["bk"] + i, _TAB_ID, 0) for i in range(2)]
             + [(_BLK["av"], _TAB_ID, 1), (_BLK["ag"], _TAB_ID, 1)]
             + [(_BLK["bv"] + i, _TAB_ID, 1) for i in range(2)] + [(_BLK["bg"] + i, _TAB_ID, 1) for i in range(2)])
_DIL_PLAN = [(_BLK[s] + g, (g if s != "av" else _TAB_ID), g - 1) for g in (1, 2) for s in ("aq", "ak", "av")]
_NAT_HEAD = {"aq": 0, "ak": 8, "bq": 16, "bk": 32, "av": 48, "ag": 56, "bv": 64, "bg": 80}
_DIL_HEAD = {(1, "aq"): 0, (1, "ak"): 8, (1, "av"): 16, (2, "aq"): 24, (2, "ak"): 32, (2, "av"): 40}


def _gate_proj_kernel(h_ref, w_ref, b_ref, o_ref):
    z = _dot_nt(h_ref[...], w_ref[...]) + b_ref[...]
    o_ref[...] = _sigmoid(z).astype(BF16)


def _project_gates(h, w, b):
    tm, tn = PROJ_TM, PROJ_TN
    n = 2 * D_MODEL
    return pl.pallas_call(
        _gate_proj_kernel,
        grid=(N_TOK // tm, n // tn),
        in_specs=[pl.BlockSpec((tm, D_MODEL), lambda i, j: (i, 0)),
                  pl.BlockSpec((pl.Element(tn), pl.Element(D_MODEL)),
                               lambda i, j: (pl.multiple_of(M_G0 + j * tn, B_HEADS), 0)),
                  pl.BlockSpec((1, tn), lambda i, j: (0, j))],
        out_specs=pl.BlockSpec((tm, tn), lambda i, j: (i, j)),
        out_shape=jax.ShapeDtypeStruct((N_TOK, n), BF16),
        compiler_params=_params(("arbitrary", "arbitrary")),
        name="proj_gates",
    )(h, w, b.reshape(1, n))


def _forget_kernel(h_ref, w_ref, b_ref, f_ref, carry_ref, *, tc):
    @pl.when(pl.program_id(1) == 0)
    def _():
        carry_ref[...] = jnp.zeros_like(carry_ref)

    z = _dot_nt(h_ref[...], w_ref[...]) + b_ref[...]
    x = jnp.minimum(z, 0.0) - jnp.log1p(jnp.exp(-jnp.abs(z)))
    row = lax.broadcasted_iota(jnp.int32, x.shape, 0)
    shift = 1
    while shift < tc:
        x = x + jnp.where(row >= shift, pltpu.roll(x, shift, 0), 0.0)
        shift *= 2
    x = x + carry_ref[...]
    f_ref[...] = x
    carry_ref[...] = x[tc - 1:tc, :]


def _forget_cumsum(h, w_f, b_f):
    tc = 512
    per_b = SEQ // tc
    return pl.pallas_call(
        functools.partial(_forget_kernel, tc=tc),
        grid=(BATCH, per_b),
        in_specs=[pl.BlockSpec((tc, D_MODEL), lambda b, c: (b * per_b + c, 0)),
                  pl.BlockSpec((LANES, D_MODEL), lambda b, c: (B_F0 // LANES, 0)),
                  pl.BlockSpec((1, LANES), lambda b, c: (0, 0))],
        out_specs=pl.BlockSpec((tc, LANES), lambda b, c: (b * per_b + c, 0)),
        out_shape=jax.ShapeDtypeStruct((N_TOK, LANES), F32),
        scratch_shapes=[pltpu.VMEM((1, LANES), F32)],
        compiler_params=_params(("arbitrary", "arbitrary")),
        name="forget_cumsum",
    )(h, w_f, b_f)


def _perm_rows(dil, r, l0, n):
    per_tile = PROJ_TM // dil
    pieces = []
    while n > 0:
        c, off = divmod(l0, per_tile)
        take = min(n, per_tile - off)
        pieces.append((c * PROJ_TM + r * per_tile + off, take))
        l0 += take
        n -= take
    return pieces


def _load_rows(ref, pieces):
    parts = [ref[s:s + n, :] for s, n in pieces]
    return parts[0] if len(parts) == 1 else jnp.concatenate(parts, axis=0)


def _attn_a_kernel(q0, k0, v0, q1, k1, v1, q2, k2, v2, gate_ref, ya_ref, va0, va1, va2, o_s, l_s):
    qi = lax.broadcasted_iota(jnp.int32, (SPAN, 2 * SPAN), 0)
    kj = lax.broadcasted_iota(jnp.int32, (SPAN, 2 * SPAN), 1)
    dist = qi - kj + SPAN
    band = (dist >= 0) & (dist <= SPAN)
    tri = (lax.broadcasted_iota(jnp.int32, (SPAN, SPAN), 0)
           >= lax.broadcasted_iota(jnp.int32, (SPAN, SPAN), 1))

    for g, (q_ref, k_ref, v_ref, vaug_s) in enumerate(((q0, k0, v0, va0), (q1, k1, v1, va1), (q2, k2, v2, va2))):
        dil = DILS[g]
        nb = SEQ // dil // SPAN
        vaug_s[:, HEAD_DIM:] = jnp.ones((SEQ, HEAD_DIM), BF16)
        vaug_s[:, :HEAD_DIM] = v_ref[...]
        for r in range(dil):
            for blk in range(nb):
                cur = _perm_rows(dil, r, blk * SPAN, SPAN)
                keys = (_perm_rows(dil, r, (blk - 1) * SPAN, SPAN) if blk > 0 else []) + cur
                qb = _load_rows(q_ref, cur)
                kb = _load_rows(k_ref, keys)
                vb = _load_rows(vaug_s, keys)
                t = lax.dot_general(qb, kb, (((1,), (1,)), ((), ())), preferred_element_type=F32) * C1
                t = jnp.where(band if blk > 0 else tri, t, -jnp.inf)
                m = jnp.max(t, axis=-1, keepdims=True)
                p = jnp.exp2(t - m)
                pv = jnp.dot(p.astype(BF16), vb, preferred_element_type=F32)
                den = pv[:, HEAD_DIM:]
                o = pv[:, :HEAD_DIM] / den
                lse2 = m + jnp.log2(den)
                if dil == 1:
                    rows = pl.ds(g * SEQ + blk * SPAN, SPAN)
                else:
                    rows = pl.ds(g * SEQ + blk * SPAN * dil + r, SPAN, stride=dil)
                o_s[rows, :] = o
                l_s[rows, :] = lse2

    ch = 256
    for c in range(SEQ // ch):
        ls = [l_s[g * SEQ + c * ch:g * SEQ + (c + 1) * ch, :] for g in range(N_DIL_GROUPS)]
        os_ = [o_s[g * SEQ + c * ch:g * SEQ + (c + 1) * ch, :] for g in range(N_DIL_GROUPS)]
        m = jnp.maximum(jnp.maximum(ls[0], ls[1]), ls[2])
        es = [jnp.exp2(x - m) for x in ls]
        inv = 1.0 / (es[0] + es[1] + es[2])
        ya = (es[0] * inv) * os_[0] + (es[1] * inv) * os_[1] + (es[2] * inv) * os_[2]
        ya_ref[c * ch:(c + 1) * ch, :] = (ya * gate_ref[c * ch:(c + 1) * ch, :].astype(F32)).astype(BF16)


def _attn_a(zn, zd):
    def spec(head0):
        return pl.BlockSpec((None, None, SEQ, HEAD_DIM), lambda b, h: (b, head0 + h, 0, 0))

    in_specs = [spec(_NAT_HEAD[s]) for s in ("aq", "ak", "av")]
    in_specs += [spec(_DIL_HEAD[(g, s)]) for g in (1, 2) for s in ("aq", "ak", "av")]
    in_specs.append(spec(_NAT_HEAD["ag"]))
    return pl.pallas_call(
        _attn_a_kernel,
        grid=(BATCH, A_HEADS),
        in_specs=in_specs,
        out_specs=pl.BlockSpec((None, SEQ, HEAD_DIM), lambda b, h: (b, 0, h)),
        out_shape=jax.ShapeDtypeStruct((BATCH, SEQ, A_WIDTH), BF16),
        scratch_shapes=[pltpu.VMEM((SEQ, 2 * HEAD_DIM), BF16)] * N_DIL_GROUPS + [
                        pltpu.VMEM((N_DIL_GROUPS * SEQ, HEAD_DIM), F32),
                        pltpu.VMEM((N_DIL_GROUPS * SEQ, HEAD_DIM), F32)],
        compiler_params=_params(("arbitrary", "arbitrary")),
        name="attn_a",
    )(zn, zn, zn, zd, zd, zd, zd, zd, zd, zn)


def _fox_kernel(q_ref, k_ref, v_ref, gate_ref, f_ref, y_ref, vaug_s, s_s, *, t):
    nq = SEQ // t
    vaug_s[:, :HEAD_DIM] = v_ref[...]
    vaug_s[:, HEAD_DIM:] = jnp.ones((SEQ, HEAD_DIM), BF16)
    kw = 2 * t

    shifts = {}

    def chunks(i):
        n_keys = (i + 1) * t
        return [(c0, min(kw, n_keys - c0)) for c0 in range(0, n_keys, kw)]

    def first_pass(i):
        qb = q_ref[i * t:(i + 1) * t, :]
        f2q = f_ref[:, i * t:(i + 1) * t] * LOG2E
        fq = jnp.transpose(jnp.broadcast_to(f2q, (LANES, t)))[:, :1]
        buf = (i % 2) * SEQ
        m = None
        for c0, w in chunks(i):
            kb = k_ref[c0:c0 + w, :]
            f2k = f_ref[:, c0:c0 + w] * LOG2E
            s = lax.dot_general(qb, kb, (((1,), (1,)), ((), ())), preferred_element_type=F32) * C1 - f2k
            if c0 + w == (i + 1) * t:
                row = lax.broadcasted_iota(jnp.int32, (t, w), 0) + i * t
                col = lax.broadcasted_iota(jnp.int32, (t, w), 1) + c0
                s = jnp.where(col <= row, s, -jnp.inf)
            s_s[:, buf + c0:buf + c0 + w] = s
            rm = jnp.max(s, axis=-1, keepdims=True)
            m = rm if m is None else jnp.maximum(m, rm)
            yield
        shifts[i] = fq - (m + fq)

    def second_pass(i):
        buf = (i % 2) * SEQ
        acc = None
        for c0, w in chunks(i):
            p = jnp.exp2(s_s[:, buf + c0:buf + c0 + w] + shifts[i])
            pv = jnp.dot(p.astype(BF16), vaug_s[c0:c0 + w, :], preferred_element_type=F32)
            acc = pv if acc is None else acc + pv
            yield
        o = acc[:, :HEAD_DIM] / acc[:, HEAD_DIM:]
        y_ref[i * t:(i + 1) * t, :] = (o * gate_ref[i * t:(i + 1) * t, :].astype(F32)).astype(BF16)

    for _ in first_pass(0):
        pass
    for i in range(nq):
        live = [second_pass(i)] + ([first_pass(i + 1)] if i + 1 < nq else [])
        while live:
            live = [g for g in live if next(g, StopIteration) is not StopIteration]


def _fox_attention(zn, f_row):
    spec = lambda s: pl.BlockSpec((None, None, SEQ, HEAD_DIM), lambda b, h: (b, _NAT_HEAD[s] + h, 0, 0))
    return pl.pallas_call(
        functools.partial(_fox_kernel, t=256),
        grid=(BATCH, B_HEADS),
        in_specs=[spec("bq"), spec("bk"), spec("bv"), spec("bg"),
                  pl.BlockSpec((None, None, 1, SEQ), lambda b, h: (b, h, 0, 0))],
        out_specs=pl.BlockSpec((None, SEQ, HEAD_DIM), lambda b, h: (b, 0, h)),
        out_shape=jax.ShapeDtypeStruct((BATCH, SEQ, B_WIDTH), BF16),
        scratch_shapes=[pltpu.VMEM((SEQ, 2 * HEAD_DIM), BF16), pltpu.VMEM((256, 2 * SEQ), F32)],
        compiler_params=_params(("arbitrary", "arbitrary")),
        name="fox_attention",
    )(zn, zn, zn, zn, f_row)


def _merge_kernel(ya_ref, yb_ref, wa_ref, wb_ref, ga_ref, gb_ref, o_ref):
    pa = jnp.dot(ya_ref[...], wa_ref[...], preferred_element_type=F32)
    pb = jnp.dot(yb_ref[...], wb_ref[...], preferred_element_type=F32)
    o_ref[...] = (ga_ref[...].astype(F32) * pa + gb_ref[...].astype(F32) * pb).astype(BF16)


def _merge(ya, yb, wa, wb, gates):
    tm, tn = 1024, 1024
    nb = D_MODEL // tn
    return pl.pallas_call(
        _merge_kernel,
        grid=(N_TOK // tm, nb),
        in_specs=[pl.BlockSpec((tm, A_WIDTH), lambda i, j: (i, 0)),
                  pl.BlockSpec((tm, B_WIDTH), lambda i, j: (i, 0)),
                  pl.BlockSpec((A_WIDTH, tn), lambda i, j: (0, j)),
                  pl.BlockSpec((B_WIDTH, tn), lambda i, j: (0, j)),
                  pl.BlockSpec((tm, tn), lambda i, j: (i, j)),
                  pl.BlockSpec((tm, tn), lambda i, j: (i, nb + j))],
        out_specs=pl.BlockSpec((tm, tn), lambda i, j: (i, j)),
        out_shape=jax.ShapeDtypeStruct((N_TOK, D_MODEL), BF16),
        compiler_params=_params(("arbitrary", "arbitrary")),
        name="merge",
    )(ya, yb, wa, wb, gates, gates)


def _out_kernel(m_ref, w_ref, x_ref, gate_ref, o_ref):
    out = jnp.dot(m_ref[...], w_ref[...], preferred_element_type=F32)
    o_ref[...] = x_ref[...] + gate_ref[...] * out


def _out_proj(merged, w_o, x2, gate):
    tm, tn = 1024, 512
    per_b = SEQ // tm
    return pl.pallas_call(
        _out_kernel,
        grid=(N_TOK // tm, D_MODEL // tn),
        in_specs=[pl.BlockSpec((tm, D_MODEL), lambda i, j: (i, 0)),
                  pl.BlockSpec((D_MODEL, tn), lambda i, j: (0, j)),
                  pl.BlockSpec((tm, tn), lambda i, j: (i, j)),
                  pl.BlockSpec((None, 1, tn), lambda i, j: (i // per_b, 0, j))],
        out_specs=pl.BlockSpec((tm, tn), lambda i, j: (i, j)),
        out_shape=jax.ShapeDtypeStruct((N_TOK, D_MODEL), F32),
        compiler_params=_params(("arbitrary", "arbitrary")),
        name="out_proj",
    )(merged, w_o, x2, gate.reshape(BATCH, 1, D_MODEL))


def _rope_tables():
    half = HEAD_DIM // 2
    inv = ROPE_THETA ** (-jnp.arange(half, dtype=F32) / half)
    ang = jnp.arange(SEQ).astype(F32)[:, None] * inv[None, :]
    cos, sin = jnp.cos(ang), jnp.sin(ang)
    cos = jnp.concatenate([cos, cos], axis=-1)
    sin = jnp.concatenate([-sin, sin], axis=-1)

    def perm(t, dil):
        t = t.reshape(SEQ // PROJ_TM, PROJ_TM // dil, dil, HEAD_DIM)
        return jnp.transpose(t, (0, 2, 1, 3)).reshape(SEQ, HEAD_DIM)

    return (jnp.stack([perm(cos, d) for d in DILS] + [jnp.ones_like(cos)]),
            jnp.stack([perm(sin, d) for d in DILS] + [jnp.zeros_like(sin)]))


def _head_rows(aq, ak, bq, bk):
    ones = jnp.ones((HEAD_DIM,), F32)
    gains = ([aq[g] for g in range(N_DIL_GROUPS) for _ in range(A_HEADS)]
             + [ak[g] for g in range(N_DIL_GROUPS) for _ in range(A_HEADS)]
             + [ones] * ((N_DIL_GROUPS + 1) * A_HEADS)
             + [bq] * B_HEADS + [bk] * B_HEADS + [ones] * (2 * B_HEADS))
    flag = np.zeros((B_F0,), np.float32)
    flag[A_Q0:A_V0] = 1.0
    flag[A_G0:B_Q0] = 1.0
    flag[B_G0:B_F0] = 1.0
    return jnp.concatenate(gains).reshape(1, B_F0), jnp.asarray(flag).reshape(1, B_F0)


def kernel(x, c, norm_g, w_ada, b_ada, w_in, b_in, a_q_norm, a_k_norm, b_q_norm, b_k_norm,
           w_a_out, w_b_out, w_o):
    depth = norm_g.shape[0]
    cos_t, sin_t = _rope_tables()
    nat_meta = jnp.asarray(np.array([p + (0,) for p in _NAT_PLAN], np.int32).T)
    dil_meta = jnp.asarray(np.array([p + (0,) for p in _DIL_PLAN], np.int32).T)
    xs = x.reshape(N_TOK, D_MODEL)
    for l in range(depth):
        mod = _modulation(c, w_ada[l], b_ada[l])
        shift, scale, gate = mod[:, :D_MODEL], mod[:, D_MODEL:2 * D_MODEL], mod[:, 2 * D_MODEL:]
        h = _norm_modulate(xs, norm_g[l], scale, shift)

        w_t = jnp.swapaxes(w_in[l], 0, 1).astype(BF16)
        w_heads, b_heads = w_t, b_in[l, :B_F0].reshape(1, B_F0)
        gains, flags = _head_rows(a_q_norm[l], a_k_norm[l], b_q_norm[l], b_k_norm[l])
        zn = _project_heads(h, w_heads, b_heads, gains, flags, cos_t, sin_t, nat_meta,
                            (("qk", 1), ("act", 1)), "proj_heads_nat")
        zd = _project_heads(h, w_heads, b_heads, gains, flags, cos_t, sin_t, dil_meta,
                            (("gen", DILS[1]), ("gen", DILS[2])), "proj_heads_dil")
        gates = _project_gates(h, w_t, b_in[l, M_G0:])
        f_cum = _forget_cumsum(h, w_t, b_in[l, B_F0:B_F0 + LANES].reshape(1, LANES))

        ya = _attn_a(zn, zd)
        f_row = jnp.transpose(f_cum.reshape(BATCH, SEQ, LANES)[:, :, :B_HEADS], (0, 2, 1))[:, :, None, :]
        yb = _fox_attention(zn, f_row)

        merged = _merge(ya.reshape(N_TOK, A_WIDTH), yb.reshape(N_TOK, B_WIDTH),
                        w_a_out[l].astype(BF16), w_b_out[l].astype(BF16), gates)
        xs = _out_proj(merged, w_o[l].astype(BF16), xs, gate)
    return xs.reshape(BATCH, SEQ, D_MODEL)
```

```python
import functools
import math

import jax
import jax.numpy as jnp
import numpy as np
from jax import lax
from jax.experimental import pallas as pl
from jax.experimental.pallas import tpu as pltpu

F32 = jnp.float32
BF16 = jnp.bfloat16

D_MODEL = 4096
BATCH = 4
SEQ = 4096
HEAD_DIM = 128
ROPE_THETA = 10000.0
EPS = 1e-6
DIL_PATTERNS = ((128, 1), (512, 4), (2048, 16))
DILS = tuple(d for _, d in DIL_PATTERNS)
N_DIL_GROUPS = 3
A_HEADS = 8
A_WIDTH = A_HEADS * HEAD_DIM
A_QKV_WIDTH = N_DIL_GROUPS * A_WIDTH
B_HEADS = 16
B_WIDTH = B_HEADS * HEAD_DIM
A_Q0 = 0
A_K0 = A_Q0 + A_QKV_WIDTH
A_V0 = A_K0 + A_QKV_WIDTH
A_G0 = A_V0 + A_QKV_WIDTH
B_Q0 = A_G0 + A_WIDTH
B_K0 = B_Q0 + B_WIDTH
B_V0 = B_K0 + B_WIDTH
B_G0 = B_V0 + B_WIDTH
B_F0 = B_G0 + B_WIDTH
M_G0 = B_F0 + B_HEADS
IN_COLS = M_G0 + 2 * D_MODEL
N_TOK = BATCH * SEQ
SPAN = 128
LANES = 128
VMEM_LIMIT = 56 * 1024 * 1024
PROJ_TM = 1024
PROJ_TN = 1024
HEADS_PER_BLOCK = PROJ_TN // HEAD_DIM
LOG2E = math.log2(math.e)
C1 = (HEAD_DIM ** -0.5) * LOG2E


def _params(sem, vmem=VMEM_LIMIT):
    return pltpu.CompilerParams(dimension_semantics=sem, vmem_limit_bytes=vmem)


def _dot_nt(a, bt):
    return lax.dot_general(a, bt, (((1,), (1,)), ((), ())), preferred_element_type=F32)


def _sigmoid(x):
    return 1.0 / (1.0 + jnp.exp(-x))


def _silu(x):
    return x * _sigmoid(x)


def _mod_kernel(c_ref, w_ref, b_ref, o_ref):
    s = _silu(c_ref[...]).astype(BF16)
    o_ref[...] = jnp.dot(s, w_ref[...].astype(BF16), preferred_element_type=F32) + b_ref[...]


def _modulation(c, w_ada, b_ada):
    tn = 512
    n = w_ada.shape[1]
    cp = jnp.pad(c, ((0, 8 - BATCH), (0, 0)))
    out = pl.pallas_call(
        _mod_kernel,
        grid=(n // tn,),
        in_specs=[pl.BlockSpec((8, D_MODEL), lambda j: (0, 0)),
                  pl.BlockSpec((D_MODEL, tn), lambda j: (0, j)),
                  pl.BlockSpec((1, tn), lambda j: (0, j))],
        out_specs=pl.BlockSpec((8, tn), lambda j: (0, j)),
        out_shape=jax.ShapeDtypeStruct((8, n), F32),
        compiler_params=_params(("arbitrary",)),
        name="modulation",
    )(cp, w_ada, b_ada.reshape(1, n))
    return out[:BATCH]


def _norm_kernel(x_ref, g_ref, scale_ref, shift_ref, h_ref):
    x = x_ref[...]
    ms = jnp.mean(x * x, axis=-1, keepdims=True)
    y = x * lax.rsqrt(ms + EPS) * g_ref[...]
    h_ref[...] = (y * (1.0 + scale_ref[...]) + shift_ref[...]).astype(BF16)


def _norm_modulate(x2, g, scale, shift):
    tm = 512
    per_b = SEQ // tm
    return pl.pallas_call(
        _norm_kernel,
        grid=(N_TOK // tm,),
        in_specs=[pl.BlockSpec((tm, D_MODEL), lambda i: (i, 0)),
                  pl.BlockSpec((1, D_MODEL), lambda i: (0, 0)),
                  pl.BlockSpec((None, 1, D_MODEL), lambda i: (i // per_b, 0, 0)),
                  pl.BlockSpec((None, 1, D_MODEL), lambda i: (i // per_b, 0, 0))],
        out_specs=pl.BlockSpec((tm, D_MODEL), lambda i: (i, 0)),
        out_shape=jax.ShapeDtypeStruct((N_TOK, D_MODEL), BF16),
        compiler_params=_params(("arbitrary",)),
        name="norm_modulate",
    )(x2, g.reshape(1, D_MODEL), scale.reshape(BATCH, 1, D_MODEL), shift.reshape(BATCH, 1, D_MODEL))


def _head_proj_kernel(meta_ref, h_ref, w_ref, b_ref, gain_ref, flag_ref, cos_ref, sin_ref, o_ref, perm_s,
                      *, bodies):
    j = pl.program_id(1)
    tm = h_ref.shape[0]
    chunk = 256

    def epilogue(x, hh, rows, kind):
        lanes = slice(hh * HEAD_DIM, (hh + 1) * HEAD_DIM)
        if kind == "act":
            return jnp.where(flag_ref[:, lanes] > 0.5, _silu(x), x).astype(BF16)
        ms = jnp.mean(x * x, axis=-1, keepdims=True)
        y = x * lax.rsqrt(ms + EPS) * gain_ref[:, lanes]
        if kind == "gen":
            y = jnp.where(flag_ref[:, lanes] > 0.5, y, x)
        y = y * cos_ref[rows, :] + pltpu.roll(y, HEAD_DIM // 2, 1) * sin_ref[rows, :]
        return y.astype(BF16)

    def run(kind, dil):
        acc = _dot_nt(h_ref[...], w_ref[...]) + b_ref[...]
        for hh in range(HEADS_PER_BLOCK):
            lanes = slice(hh * HEAD_DIM, (hh + 1) * HEAD_DIM)
            if kind == "act":
                for c0 in range(0, tm, chunk):
                    rows = slice(c0, c0 + chunk)
                    o_ref[hh, rows, :] = epilogue(acc[rows, lanes], hh, rows, kind)
                continue
            perm_s[hh * tm:(hh + 1) * tm, :] = acc[:, lanes]
            if dil == 1:
                for c0 in range(0, tm, chunk):
                    rows = slice(c0, c0 + chunk)
                    start = pl.multiple_of(hh * tm + c0 + meta_ref[3, j], chunk)
                    o_ref[hh, rows, :] = epilogue(perm_s[pl.ds(start, chunk), :], hh, rows, kind)
            else:
                n = tm // dil
                for r in range(dil):
                    rows = slice(r * n, (r + 1) * n)
                    xr = perm_s[pl.ds(hh * tm + r, n, stride=dil), :]
                    o_ref[hh, rows, :] = epilogue(xr, hh, rows, kind)

    for bid, (kind, dil) in enumerate(bodies):
        pl.when(meta_ref[2, j] == bid)(functools.partial(run, kind, dil))


def _project_heads(h, w, b, gain, flag, cos_t, sin_t, meta, bodies, name):
    tm, tn = PROJ_TM, PROJ_TN
    nj = meta.shape[1]
    per_b = SEQ // tm
    col = lambda i, j, m: (0, m[0, j])
    wrow = lambda i, j, m: (m[0, j], 0)
    tab = lambda i, j, m: (m[1, j], i % per_b, 0)
    grid_spec = pltpu.PrefetchScalarGridSpec(
        num_scalar_prefetch=1,
        grid=(N_TOK // tm, nj),
        in_specs=[pl.BlockSpec((tm, D_MODEL), lambda i, j, m: (i, 0)),
                  pl.BlockSpec((tn, D_MODEL), wrow),
                  pl.BlockSpec((1, tn), col),
                  pl.BlockSpec((1, tn), col),
                  pl.BlockSpec((1, tn), col),
                  pl.BlockSpec((None, tm, HEAD_DIM), tab),
                  pl.BlockSpec((None, tm, HEAD_DIM), tab)],
        out_specs=pl.BlockSpec((None, HEADS_PER_BLOCK, tm, HEAD_DIM),
                               lambda i, j, m: (i // per_b, j, i % per_b, 0)),
        scratch_shapes=[pltpu.VMEM((HEADS_PER_BLOCK * tm, HEAD_DIM), F32)],
    )
    return pl.pallas_call(
        functools.partial(_head_proj_kernel, bodies=bodies),
        grid_spec=grid_spec,
        out_shape=jax.ShapeDtypeStruct((BATCH, nj * HEADS_PER_BLOCK, SEQ, HEAD_DIM), BF16),
        compiler_params=_params(("arbitrary", "arbitrary")),
        name=name,
    )(meta, h, w, b, gain, flag, cos_t, sin_t)


_BLK = {"aq": 0, "ak": 3, "av": 6, "ag": 9, "bq": 10, "bk": 12, "bv": 14, "bg": 16}
_TAB_ID = 3
_NAT_PLAN = ([(_BLK["aq"], 0, 0), (_BLK["ak"], 0, 0)]
             + [(_BLK["bq"] + i, _TAB_ID, 0) for i in range(2)] + [(_BLK["bk"] + i, _TAB_ID, 0) for i in range(2)]
             + [(_BLK["av"], _TAB_ID, 1), (_BLK["ag"], _TAB_ID, 1)]
             + [(_BLK["bv"] + i, _TAB_ID, 1) for i in range(2)] + [(_BLK["bg"] + i, _TAB_ID, 1) for i in range(2)])
_DIL_PLAN = [(_BLK[s] + g, (g if s != "av" else _TAB_ID), g - 1) for g in (1, 2) for s in ("aq", "ak", "av")]
_NAT_HEAD = {"aq": 0, "ak": 8, "bq": 16, "bk": 32, "av": 48, "ag": 56, "bv": 64, "bg": 80}
_DIL_HEAD = {(1, "aq"): 0, (1, "ak"): 8, (1, "av"): 16, (2, "aq"): 24, (2, "ak"): 32, (2, "av"): 40}


def _gate_proj_kernel(h_ref, w_ref, b_ref, o_ref):
    z = _dot_nt(h_ref[...], w_ref[...]) + b_ref[...]
    o_ref[...] = _sigmoid(z).astype(BF16)


def _project_gates(h, w, b):
    tm, tn = PROJ_TM, PROJ_TN
    n = 2 * D_MODEL
    return pl.pallas_call(
        _gate_proj_kernel,
        grid=(N_TOK // tm, n // tn),
        in_specs=[pl.BlockSpec((tm, D_MODEL), lambda i, j: (i, 0)),
                  pl.BlockSpec((pl.Element(tn), pl.Element(D_MODEL)),
                               lambda i, j: (pl.multiple_of(M_G0 + j * tn, B_HEADS), 0)),
                  pl.BlockSpec((1, tn), lambda i, j: (0, j))],
        out_specs=pl.BlockSpec((tm, tn), lambda i, j: (i, j)),
        out_shape=jax.ShapeDtypeStruct((N_TOK, n), BF16),
        compiler_params=_params(("arbitrary", "arbitrary")),
        name="proj_gates",
    )(h, w, b.reshape(1, n))


def _forget_kernel(h_ref, w_ref, b_ref, f_ref, carry_ref, *, tc):
    @pl.when(pl.program_id(1) == 0)
    def _():
        carry_ref[...] = jnp.zeros_like(carry_ref)

    z = _dot_nt(h_ref[...], w_ref[...]) + b_ref[...]
    x = jnp.minimum(z, 0.0) - jnp.log1p(jnp.exp(-jnp.abs(z)))
    row = lax.broadcasted_iota(jnp.int32, x.shape, 0)
    shift = 1
    while shift < tc:
        x = x + jnp.where(row >= shift, pltpu.roll(x, shift, 0), 0.0)
        shift *= 2
    x = x + carry_ref[...]
    f_ref[...] = x
    carry_ref[...] = x[tc - 1:tc, :]


def _forget_cumsum(h, w_f, b_f):
    tc = 512
    per_b = SEQ // tc
    return pl.pallas_call(
        functools.partial(_forget_kernel, tc=tc),
        grid=(BATCH, per_b),
        in_specs=[pl.BlockSpec((tc, D_MODEL), lambda b, c: (b * per_b + c, 0)),
                  pl.BlockSpec((LANES, D_MODEL), lambda b, c: (B_F0 // LANES, 0)),
                  pl.BlockSpec((1, LANES), lambda b, c: (0, 0))],
        out_specs=pl.BlockSpec((tc, LANES), lambda b, c: (b * per_b + c, 0)),
        out_shape=jax.ShapeDtypeStruct((N_TOK, LANES), F32),
        scratch_shapes=[pltpu.VMEM((1, LANES), F32)],
        compiler_params=_params(("arbitrary", "arbitrary")),
        name="forget_cumsum",
    )(h, w_f, b_f)


def _perm_rows(dil, r, l0, n):
    per_tile = PROJ_TM // dil
    pieces = []
    while n > 0:
        c, off = divmod(l0, per_tile)
        take = min(n, per_tile - off)
        pieces.append((c * PROJ_TM + r * per_tile + off, take))
        l0 += take
        n -= take
    return pieces


def _load_rows(ref, pieces):
    parts = [ref[s:s + n, :] for s, n in pieces]
    return parts[0] if len(parts) == 1 else jnp.concatenate(parts, axis=0)


def _attn_a_kernel(q0, k0, v0, q1, k1, v1, q2, k2, v2, gate_ref, ya_ref, va0, va1, va2, o_s, l_s):
    qi = lax.broadcasted_iota(jnp.int32, (SPAN, 2 * SPAN), 0)
    kj = lax.broadcasted_iota(jnp.int32, (SPAN, 2 * SPAN), 1)
    dist = qi - kj + SPAN
    band = (dist >= 0) & (dist <= SPAN)
    tri = (lax.broadcasted_iota(jnp.int32, (SPAN, SPAN), 0)
           >= lax.broadcasted_iota(jnp.int32, (SPAN, SPAN), 1))

    for g, (q_ref, k_ref, v_ref, vaug_s) in enumerate(((q0, k0, v0, va0), (q1, k1, v1, va1), (q2, k2, v2, va2))):
        dil = DILS[g]
        nb = SEQ // dil // SPAN
        vaug_s[:, HEAD_DIM:] = jnp.ones((SEQ, HEAD_DIM), BF16)
        vaug_s[:, :HEAD_DIM] = v_ref[...]
        for r in range(dil):
            for blk in range(nb):
                cur = _perm_rows(dil, r, blk * SPAN, SPAN)
                keys = (_perm_rows(dil, r, (blk - 1) * SPAN, SPAN) if blk > 0 else []) + cur
                qb = _load_rows(q_ref, cur)
                kb = _load_rows(k_ref, keys)
                vb = _load_rows(vaug_s, keys)
                t = lax.dot_general(qb, kb, (((1,), (1,)), ((), ())), preferred_element_type=F32) * C1
                t = jnp.where(band if blk > 0 else tri, t, -jnp.inf)
                m = jnp.max(t, axis=-1, keepdims=True)
                p = jnp.exp2(t - m)
                pv = jnp.dot(p.astype(BF16), vb, preferred_element_type=F32)
                den = pv[:, HEAD_DIM:]
                o = pv[:, :HEAD_DIM] / den
                lse2 = m + jnp.log2(den)
                if dil == 1:
                    rows = pl.ds(g * SEQ + blk * SPAN, SPAN)
                else:
                    rows = pl.ds(g * SEQ + blk * SPAN * dil + r, SPAN, stride=dil)
                o_s[rows, :] = o
                l_s[rows, :] = lse2

    ch = 256
    for c in range(SEQ // ch):
        ls = [l_s[g * SEQ + c * ch:g * SEQ + (c + 1) * ch, :] for g in range(N_DIL_GROUPS)]
        os_ = [o_s[g * SEQ + c * ch:g * SEQ + (c + 1) * ch, :] for g in range(N_DIL_GROUPS)]
        m = jnp.maximum(jnp.maximum(ls[0], ls[1]), ls[2])
        es = [jnp.exp2(x - m) for x in ls]
        inv = 1.0 / (es[0] + es[1] + es[2])
        ya = (es[0] * inv) * os_[0] + (es[1] * inv) * os_[1] + (es[2] * inv) * os_[2]
        ya_ref[c * ch:(c + 1) * ch, :] = (ya * gate_ref[c * ch:(c + 1) * ch, :].astype(F32)).astype(BF16)


def _attn_a(zn, zd):
    def spec(head0):
        return pl.BlockSpec((None, None, SEQ, HEAD_DIM), lambda b, h: (b, head0 + h, 0, 0))

    in_specs = [spec(_NAT_HEAD[s]) for s in ("aq", "ak", "av")]
    in_specs += [spec(_DIL_HEAD[(g, s)]) for g in (1, 2) for s in ("aq", "ak", "av")]
    in_specs.append(spec(_NAT_HEAD["ag"]))
    return pl.pallas_call(
        _attn_a_kernel,
        grid=(BATCH, A_HEADS),
        in_specs=in_specs,
        out_specs=pl.BlockSpec((None, SEQ, HEAD_DIM), lambda b, h: (b, 0, h)),
        out_shape=jax.ShapeDtypeStruct((BATCH, SEQ, A_WIDTH), BF16),
        scratch_shapes=[pltpu.VMEM((SEQ, 2 * HEAD_DIM), BF16)] * N_DIL_GROUPS + [
                        pltpu.VMEM((N_DIL_GROUPS * SEQ, HEAD_DIM), F32),
                        pltpu.VMEM((N_DIL_GROUPS * SEQ, HEAD_DIM), F32)],
        compiler_params=_params(("arbitrary", "arbitrary")),
        name="attn_a",
    )(zn, zn, zn, zd, zd, zd, zd, zd, zd, zn)


def _fox_kernel(q_ref, k_ref, v_ref, gate_ref, f_ref, y_ref, vaug_s, s_s, *, t):
    nq = SEQ // t
    vaug_s[:, :HEAD_DIM] = v_ref[...]
    vaug_s[:, HEAD_DIM:] = jnp.ones((SEQ, HEAD_DIM), BF16)
    kw = 2 * t

    shifts = {}

    def chunks(i):
        n_keys = (i + 1) * t
        return [(c0, min(kw, n_keys - c0)) for c0 in range(0, n_keys, kw)]

    def first_pass(i):
        qb = q_ref[i * t:(i + 1) * t, :]
        f2q = f_ref[:, i * t:(i + 1) * t] * LOG2E
        fq = jnp.transpose(jnp.broadcast_to(f2q, (LANES, t)))[:, :1]
        buf = (i % 2) * SEQ
        m = None
        for c0, w in chunks(i):
            kb = k_ref[c0:c0 + w, :]
            f2k = f_ref[:, c0:c0 + w] * LOG2E
            s = lax.dot_general(qb, kb, (((1,), (1,)), ((), ())), preferred_element_type=F32) * C1 - f2k
            if c0 + w == (i + 1) * t:
                row = lax.broadcasted_iota(jnp.int32, (t, w), 0) + i * t
                col = lax.broadcasted_iota(jnp.int32, (t, w), 1) + c0
                s = jnp.where(col <= row, s, -jnp.inf)
            s_s[:, buf + c0:buf + c0 + w] = s
            rm = jnp.max(s, axis=-1, keepdims=True)
            m = rm if m is None else jnp.maximum(m, rm)
            yield
        shifts[i] = fq - (m + fq)

    def second_pass(i):
        buf = (i % 2) * SEQ
        acc = None
        for c0, w in chunks(i):
            p = jnp.exp2(s_s[:, buf + c0:buf + c0 + w] + shifts[i])
            pv = jnp.dot(p.astype(BF16), vaug_s[c0:c0 + w, :], preferred_element_type=F32)
            acc = pv if acc is None else acc + pv
            yield
        o = acc[:, :HEAD_DIM] / acc[:, HEAD_DIM:]
        y_ref[i * t:(i + 1) * t, :] = (o * gate_ref[i * t:(i + 1) * t, :].astype(F32)).astype(BF16)

    for _ in first_pass(0):
        pass
    for i in range(nq):
        live = [second_pass(i)] + ([first_pass(i + 1)] if i + 1 < nq else [])
        while live:
            live = [g for g in live if next(g, StopIteration) is not StopIteration]


def _fox_attention(zn, f_row):
    t = 512
    spec = lambda s: pl.BlockSpec((None, None, SEQ, HEAD_DIM), lambda b, h: (b, _NAT_HEAD[s] + h, 0, 0))
    return pl.pallas_call(
        functools.partial(_fox_kernel, t=t),
        grid=(BATCH, B_HEADS),
        in_specs=[spec("bq"), spec("bk"), spec("bv"), spec("bg"),
                  pl.BlockSpec((None, None, 1, SEQ), lambda b, h: (b, h, 0, 0))],
        out_specs=pl.BlockSpec((None, SEQ, HEAD_DIM), lambda b, h: (b, 0, h)),
        out_shape=jax.ShapeDtypeStruct((BATCH, SEQ, B_WIDTH), BF16),
        scratch_shapes=[pltpu.VMEM((SEQ, 2 * HEAD_DIM), BF16), pltpu.VMEM((t, 2 * SEQ), F32)],
        compiler_params=_params(("arbitrary", "arbitrary")),
        name="fox_attention",
    )(zn, zn, zn, zn, f_row)


def _merge_kernel(ya_ref, yb_ref, wa_ref, wb_ref, ga_ref, gb_ref, o_ref):
    pa = jnp.dot(ya_ref[...], wa_ref[...], preferred_element_type=F32)
    pb = jnp.dot(yb_ref[...], wb_ref[...], preferred_element_type=F32)
    o_ref[...] = (ga_ref[...].astype(F32) * pa + gb_ref[...].astype(F32) * pb).astype(BF16)


def _merge(ya, yb, wa, wb, gates):
    tm, tn = 1024, 1024
    nb = D_MODEL // tn
    return pl.pallas_call(
        _merge_kernel,
        grid=(N_TOK // tm, nb),
        in_specs=[pl.BlockSpec((tm, A_WIDTH), lambda i, j: (i, 0)),
                  pl.BlockSpec((tm, B_WIDTH), lambda i, j: (i, 0)),
                  pl.BlockSpec((A_WIDTH, tn), lambda i, j: (0, j)),
                  pl.BlockSpec((B_WIDTH, tn), lambda i, j: (0, j)),
                  pl.BlockSpec((tm, tn), lambda i, j: (i, j)),
                  pl.BlockSpec((tm, tn), lambda i, j: (i, nb + j))],
        out_specs=pl.BlockSpec((tm, tn), lambda i, j: (i, j)),
        out_shape=jax.ShapeDtypeStruct((N_TOK, D_MODEL), BF16),
        compiler_params=_params(("arbitrary", "arbitrary")),
        name="merge",
    )(ya, yb, wa, wb, gates, gates)


def _out_kernel(m_ref, w_ref, x_ref, gate_ref, o_ref):
    out = jnp.dot(m_ref[...], w_ref[...], preferred_element_type=F32)
    o_ref[...] = x_ref[...] + gate_ref[...] * out


def _out_proj(merged, w_o, x2, gate):
    tm, tn = 1024, 512
    per_b = SEQ // tm
    return pl.pallas_call(
        _out_kernel,
        grid=(N_TOK // tm, D_MODEL // tn),
        in_specs=[pl.BlockSpec((tm, D_MODEL), lambda i, j: (i, 0)),
                  pl.BlockSpec((D_MODEL, tn), lambda i, j: (0, j)),
                  pl.BlockSpec((tm, tn), lambda i, j: (i, j)),
                  pl.BlockSpec((None, 1, tn), lambda i, j: (i // per_b, 0, j))],
        out_specs=pl.BlockSpec((tm, tn), lambda i, j: (i, j)),
        out_shape=jax.ShapeDtypeStruct((N_TOK, D_MODEL), F32),
        compiler_params=_params(("arbitrary", "arbitrary")),
        name="out_proj",
    )(merged, w_o, x2, gate.reshape(BATCH, 1, D_MODEL))


def _rope_tables():
    half = HEAD_DIM // 2
    inv = ROPE_THETA ** (-jnp.arange(half, dtype=F32) / half)
    ang = jnp.arange(SEQ).astype(F32)[:, None] * inv[None, :]
    cos, sin = jnp.cos(ang), jnp.sin(ang)
    cos = jnp.concatenate([cos, cos], axis=-1)
    sin = jnp.concatenate([-sin, sin], axis=-1)

    def perm(t, dil):
        t = t.reshape(SEQ // PROJ_TM, PROJ_TM // dil, dil, HEAD_DIM)
        return jnp.transpose(t, (0, 2, 1, 3)).reshape(SEQ, HEAD_DIM)

    return (jnp.stack([perm(cos, d) for d in DILS] + [jnp.ones_like(cos)]),
            jnp.stack([perm(sin, d) for d in DILS] + [jnp.zeros_like(sin)]))


def _head_rows(aq, ak, bq, bk):
    ones = jnp.ones((HEAD_DIM,), F32)
    gains = ([aq[g] for g in range(N_DIL_GROUPS) for _ in range(A_HEADS)]
             + [ak[g] for g in range(N_DIL_GROUPS) for _ in range(A_HEADS)]
             + [ones] * ((N_DIL_GROUPS + 1) * A_HEADS)
             + [bq] * B_HEADS + [bk] * B_HEADS + [ones] * (2 * B_HEADS))
    flag = np.zeros((B_F0,), np.float32)
    flag[A_Q0:A_V0] = 1.0
    flag[A_G0:B_Q0] = 1.0
    flag[B_G0:B_F0] = 1.0
    return jnp.concatenate(gains).reshape(1, B_F0), jnp.asarray(flag).reshape(1, B_F0)


def kernel(x, c, norm_g, w_ada, b_ada, w_in, b_in, a_q_norm, a_k_norm, b_q_norm, b_k_norm,
           w_a_out, w_b_out, w_o):
    depth = norm_g.shape[0]
    cos_t, sin_t = _rope_tables()
    nat_meta = jnp.asarray(np.array([p + (0,) for p in _NAT_PLAN], np.int32).T)
    dil_meta = jnp.asarray(np.array([p + (0,) for p in _DIL_PLAN], np.int32).T)
    xs = x.reshape(N_TOK, D_MODEL)
    for l in range(depth):
        mod = _modulation(c, w_ada[l], b_ada[l])
        shift, scale, gate = mod[:, :D_MODEL], mod[:, D_MODEL:2 * D_MODEL], mod[:, 2 * D_MODEL:]
        h = _norm_modulate(xs, norm_g[l], scale, shift)

        w_t = jnp.swapaxes(w_in[l], 0, 1).astype(BF16)
        w_heads, b_heads = w_t, b_in[l, :B_F0].reshape(1, B_F0)
        gains, flags = _head_rows(a_q_norm[l], a_k_norm[l], b_q_norm[l], b_k_norm[l])
        zn = _project_heads(h, w_heads, b_heads, gains, flags, cos_t, sin_t, nat_meta,
                            (("qk", 1), ("act", 1)), "proj_heads_nat")
        zd = _project_heads(h, w_heads, b_heads, gains, flags, cos_t, sin_t, dil_meta,
                            (("gen", DILS[1]), ("gen", DILS[2])), "proj_heads_dil")
        gates = _project_gates(h, w_t, b_in[l, M_G0:])
        f_cum = _forget_cumsum(h, w_t, b_in[l, B_F0:B_F0 + LANES].reshape(1, LANES))

        ya = _attn_a(zn, zd)
        f_row = jnp.transpose(f_cum.reshape(BATCH, SEQ, LANES)[:, :, :B_HEADS], (0, 2, 1))[:, :, None, :]
        yb = _fox_attention(zn, f_row)

        merged = _merge(ya.reshape(N_TOK, A_WIDTH), yb.reshape(N_TOK, B_WIDTH),
                        w_a_out[l].astype(BF16), w_b_out[l].astype(BF16), gates)
        xs = _out_proj(merged, w_o[l].astype(BF16), xs, gate)
    return xs.reshape(BATCH, SEQ, D_MODEL)
```

```python
import functools
import math

import jax
import jax.numpy as jnp
import numpy as np
from jax import lax
from jax.experimental import pallas as pl
from jax.experimental.pallas import tpu as pltpu

F32 = jnp.float32
BF16 = jnp.bfloat16

D_MODEL = 4096
BATCH = 4
SEQ = 4096
HEAD_DIM = 128
ROPE_THETA = 10000.0
EPS = 1e-6
DIL_PATTERNS = ((128, 1), (512, 4), (2048, 16))
DILS = tuple(d for _, d in DIL_PATTERNS)
N_DIL_GROUPS = 3
A_HEADS = 8
A_WIDTH = A_HEADS * HEAD_DIM
A_QKV_WIDTH = N_DIL_GROUPS * A_WIDTH
B_HEADS = 16
B_WIDTH = B_HEADS * HEAD_DIM
A_Q0 = 0
A_K0 = A_Q0 + A_QKV_WIDTH
A_V0 = A_K0 + A_QKV_WIDTH
A_G0 = A_V0 + A_QKV_WIDTH
B_Q0 = A_G0 + A_WIDTH
B_K0 = B_Q0 + B_WIDTH
B_V0 = B_K0 + B_WIDTH
B_G0 = B_V0 + B_WIDTH
B_F0 = B_G0 + B_WIDTH
M_G0 = B_F0 + B_HEADS
IN_COLS = M_G0 + 2 * D_MODEL
N_TOK = BATCH * SEQ
SPAN = 128
LANES = 128
VMEM_LIMIT = 56 * 1024 * 1024
PROJ_TM = 1024
PROJ_TN = 1024
HEADS_PER_BLOCK = PROJ_TN // HEAD_DIM
LOG2E = math.log2(math.e)
C1 = (HEAD_DIM ** -0.5) * LOG2E


def _params(sem, vmem=VMEM_LIMIT):
    return pltpu.CompilerParams(dimension_semantics=sem, vmem_limit_bytes=vmem)


def _dot_nt(a, bt):
    return lax.dot_general(a, bt, (((1,), (1,)), ((), ())), preferred_element_type=F32)


def _sigmoid(x):
    return 0.5 * jnp.tanh(0.5 * x) + 0.5


def _silu(x):
    return x * _sigmoid(x)


def _mod_kernel(c_ref, w_ref, b_ref, o_ref):
    s = _silu(c_ref[...]).astype(BF16)
    o_ref[...] = jnp.dot(s, w_ref[...].astype(BF16), preferred_element_type=F32) + b_ref[...]


def _modulation(c, w_ada, b_ada):
    tn = 512
    n = w_ada.shape[1]
    cp = jnp.pad(c, ((0, 8 - BATCH), (0, 0)))
    out = pl.pallas_call(
        _mod_kernel,
        grid=(n // tn,),
        in_specs=[pl.BlockSpec((8, D_MODEL), lambda j: (0, 0)),
                  pl.BlockSpec((D_MODEL, tn), lambda j: (0, j)),
                  pl.BlockSpec((1, tn), lambda j: (0, j))],
        out_specs=pl.BlockSpec((8, tn), lambda j: (0, j)),
        out_shape=jax.ShapeDtypeStruct((8, n), F32),
        compiler_params=_params(("arbitrary",)),
        name="modulation",
    )(cp, w_ada, b_ada.reshape(1, n))
    return out[:BATCH]


def _norm_kernel(x_ref, g_ref, scale_ref, shift_ref, wf_ref, bf_ref, h_ref, f_ref, carry_ref, *, tm, per_b):
    x = x_ref[...]
    ms = jnp.mean(x * x, axis=-1, keepdims=True)
    y = x * lax.rsqrt(ms + EPS) * g_ref[...]
    h = (y * (1.0 + scale_ref[...]) + shift_ref[...]).astype(BF16)
    h_ref[...] = h

    @pl.when(pl.program_id(0) % per_b == 0)
    def _():
        carry_ref[...] = jnp.zeros_like(carry_ref)

    z = _dot_nt(h, wf_ref[...]) + bf_ref[...]
    lf = jnp.minimum(z, 0.0) - jnp.log1p(jnp.exp(-jnp.abs(z)))
    row = lax.broadcasted_iota(jnp.int32, lf.shape, 0)
    shift = 1
    while shift < tm:
        lf = lf + jnp.where(row >= shift, pltpu.roll(lf, shift, 0), 0.0)
        shift *= 2
    lf = lf + carry_ref[...]
    f_ref[...] = lf
    carry_ref[...] = lf[tm - 1:tm, :]


def _norm_modulate(x2, g, scale, shift, w_t, b_f):
    tm = 512
    per_b = SEQ // tm
    return pl.pallas_call(
        functools.partial(_norm_kernel, tm=tm, per_b=per_b),
        grid=(N_TOK // tm,),
        in_specs=[pl.BlockSpec((tm, D_MODEL), lambda i: (i, 0)),
                  pl.BlockSpec((1, D_MODEL), lambda i: (0, 0)),
                  pl.BlockSpec((None, 1, D_MODEL), lambda i: (i // per_b, 0, 0)),
                  pl.BlockSpec((None, 1, D_MODEL), lambda i: (i // per_b, 0, 0)),
                  pl.BlockSpec((LANES, D_MODEL), lambda i: (B_F0 // LANES, 0)),
                  pl.BlockSpec((1, LANES), lambda i: (0, 0))],
        out_specs=[pl.BlockSpec((tm, D_MODEL), lambda i: (i, 0)),
                   pl.BlockSpec((tm, LANES), lambda i: (i, 0))],
        out_shape=[jax.ShapeDtypeStruct((N_TOK, D_MODEL), BF16),
                   jax.ShapeDtypeStruct((N_TOK, LANES), F32)],
        scratch_shapes=[pltpu.VMEM((1, LANES), F32)],
        compiler_params=_params(("arbitrary",)),
        name="norm_modulate",
    )(x2, g.reshape(1, D_MODEL), scale.reshape(BATCH, 1, D_MODEL), shift.reshape(BATCH, 1, D_MODEL), w_t, b_f)


def _head_proj_kernel(meta_ref, h_ref, w_ref, b_ref, gain_ref, flag_ref, cos_ref, sin_ref, o_ref, perm_s,
                      *, bodies):
    j = pl.program_id(1)
    tm = h_ref.shape[0]
    chunk = 256

    def epilogue(x, hh, rows, kind):
        lanes = slice(hh * HEAD_DIM, (hh + 1) * HEAD_DIM)
        if kind == "act":
            return jnp.where(flag_ref[:, lanes] > 0.5, _silu(x), x).astype(BF16)
        ms = jnp.mean(x * x, axis=-1, keepdims=True)
        y = x * lax.rsqrt(ms + EPS) * gain_ref[:, lanes]
        if kind == "gen":
            y = jnp.where(flag_ref[:, lanes] > 0.5, y, x)
        y = y * cos_ref[rows, :] + pltpu.roll(y, HEAD_DIM // 2, 1) * sin_ref[rows, :]
        return y.astype(BF16)

    def run(kind, dil):
        acc = _dot_nt(h_ref[...], w_ref[...]) + b_ref[...]
        for hh in range(HEADS_PER_BLOCK):
            lanes = slice(hh * HEAD_DIM, (hh + 1) * HEAD_DIM)
            if kind == "act":
                for c0 in range(0, tm, chunk):
                    rows = slice(c0, c0 + chunk)
                    o_ref[hh, rows, :] = epilogue(acc[rows, lanes], hh, rows, kind)
                continue
            perm_s[hh * tm:(hh + 1) * tm, :] = acc[:, lanes]
            if dil == 1:
                for c0 in range(0, tm, chunk):
                    rows = slice(c0, c0 + chunk)
                    start = pl.multiple_of(hh * tm + c0 + meta_ref[3, j], chunk)
                    o_ref[hh, rows, :] = epilogue(perm_s[pl.ds(start, chunk), :], hh, rows, kind)
            else:
                n = tm // dil
                for r in range(dil):
                    rows = slice(r * n, (r + 1) * n)
                    xr = perm_s[pl.ds(hh * tm + r, n, stride=dil), :]
                    o_ref[hh, rows, :] = epilogue(xr, hh, rows, kind)

    for bid, (kind, dil) in enumerate(bodies):
        pl.when(meta_ref[2, j] == bid)(functools.partial(run, kind, dil))


def _project_heads(h, w, b, gain, flag, cos_t, sin_t, meta, bodies, name):
    tm, tn = PROJ_TM, PROJ_TN
    nj = meta.shape[1]
    per_b = SEQ // tm
    col = lambda i, j, m: (0, m[0, j])
    wrow = lambda i, j, m: (m[0, j], 0)
    tab = lambda i, j, m: (m[1, j], i % per_b, 0)
    grid_spec = pltpu.PrefetchScalarGridSpec(
        num_scalar_prefetch=1,
        grid=(N_TOK // tm, nj),
        in_specs=[pl.BlockSpec((tm, D_MODEL), lambda i, j, m: (i, 0)),
                  pl.BlockSpec((tn, D_MODEL), wrow),
                  pl.BlockSpec((1, tn), col),
                  pl.BlockSpec((1, tn), col),
                  pl.BlockSpec((1, tn), col),
                  pl.BlockSpec((None, tm, HEAD_DIM), tab),
                  pl.BlockSpec((None, tm, HEAD_DIM), tab)],
        out_specs=pl.BlockSpec((None, HEADS_PER_BLOCK, tm, HEAD_DIM),
                               lambda i, j, m: (i // per_b, j, i % per_b, 0)),
        scratch_shapes=[pltpu.VMEM((HEADS_PER_BLOCK * tm, HEAD_DIM), F32)],
    )
    return pl.pallas_call(
        functools.partial(_head_proj_kernel, bodies=bodies),
        grid_spec=grid_spec,
        out_shape=jax.ShapeDtypeStruct((BATCH, nj * HEADS_PER_BLOCK, SEQ, HEAD_DIM), BF16),
        compiler_params=_params(("arbitrary", "arbitrary")),
        name=name,
    )(meta, h, w, b, gain, flag, cos_t, sin_t)


_BLK = {"aq": 0, "ak": 3, "av": 6, "ag": 9, "bq": 10, "bk": 12, "bv": 14, "bg": 16}
_TAB_ID = 3
_NAT_PLAN = ([(_BLK["aq"], 0, 0), (_BLK["ak"], 0, 0)]
             + [(_BLK["bq"] + i, _TAB_ID, 0) for i in range(2)] + [(_BLK["bk"] + i, _TAB_ID, 0) for i in range(2)]
             + [(_BLK["av"], _TAB_ID, 1), (_BLK["ag"], _TAB_ID, 1)]
             + [(_BLK["bv"] + i, _TAB_ID, 1) for i in range(2)] + [(_BLK["bg"] + i, _TAB_ID, 1) for i in range(2)])
_DIL_PLAN = [(_BLK[s] + g, (g if s != "av" else _TAB_ID), g - 1) for g in (1, 2) for s in ("aq", "ak", "av")]
_NAT_HEAD = {"aq": 0, "ak": 8, "bq": 16, "bk": 32, "av": 48, "ag": 56, "bv": 64, "bg": 80}
_DIL_HEAD = {(1, "aq"): 0, (1, "ak"): 8, (1, "av"): 16, (2, "aq"): 24, (2, "ak"): 32, (2, "av"): 40}


def _gate_proj_kernel(h_ref, w_ref, b_ref, o_ref):
    z = _dot_nt(h_ref[...], w_ref[...]) + b_ref[...]
    o_ref[...] = _sigmoid(z).astype(BF16)


def _project_gates(h, w, b):
    tm, tn = PROJ_TM, PROJ_TN
    n = 2 * D_MODEL
    return pl.pallas_call(
        _gate_proj_kernel,
        grid=(N_TOK // tm, n // tn),
        in_specs=[pl.BlockSpec((tm, D_MODEL), lambda i, j: (i, 0)),
                  pl.BlockSpec((pl.Element(tn), pl.Element(D_MODEL)),
                               lambda i, j: (pl.multiple_of(M_G0 + j * tn, B_HEADS), 0)),
                  pl.BlockSpec((1, tn), lambda i, j: (0, j))],
        out_specs=pl.BlockSpec((tm, tn), lambda i, j: (i, j)),
        out_shape=jax.ShapeDtypeStruct((N_TOK, n), BF16),
        compiler_params=_params(("arbitrary", "arbitrary")),
        name="proj_gates",
    )(h, w, b.reshape(1, n))


def _perm_rows(dil, r, l0, n):
    per_tile = PROJ_TM // dil
    pieces = []
    while n > 0:
        c, off = divmod(l0, per_tile)
        take = min(n, per_tile - off)
        pieces.append((c * PROJ_TM + r * per_tile + off, take))
        l0 += take
        n -= take
    return pieces


def _load_rows(ref, pieces):
    parts = [ref[s:s + n, :] for s, n in pieces]
    return parts[0] if len(parts) == 1 else jnp.concatenate(parts, axis=0)


def _attn_a_kernel(q0, k0, v0, q1, k1, v1, q2, k2, v2, gate_ref, ya_ref, va0, va1, va2, o_s, l_s):
    qi = lax.broadcasted_iota(jnp.int32, (SPAN, 2 * SPAN), 0)
    kj = lax.broadcasted_iota(jnp.int32, (SPAN, 2 * SPAN), 1)
    dist = qi - kj + SPAN
    band = (dist >= 0) & (dist <= SPAN)
    tri = (lax.broadcasted_iota(jnp.int32, (SPAN, SPAN), 0)
           >= lax.broadcasted_iota(jnp.int32, (SPAN, SPAN), 1))

    for g, (q_ref, k_ref, v_ref, vaug_s) in enumerate(((q0, k0, v0, va0), (q1, k1, v1, va1), (q2, k2, v2, va2))):
        dil = DILS[g]
        nb = SEQ // dil // SPAN
        vaug_s[:, HEAD_DIM:] = jnp.ones((SEQ, HEAD_DIM), BF16)
        vaug_s[:, :HEAD_DIM] = v_ref[...]
        for r in range(dil):
            for blk in range(nb):
                cur = _perm_rows(dil, r, blk * SPAN, SPAN)
                keys = (_perm_rows(dil, r, (blk - 1) * SPAN, SPAN) if blk > 0 else []) + cur
                qb = _load_rows(q_ref, cur)
                kb = _load_rows(k_ref, keys)
                vb = _load_rows(vaug_s, keys)
                t = lax.dot_general(qb, kb, (((1,), (1,)), ((), ())), preferred_element_type=F32) * C1
                t = jnp.where(band if blk > 0 else tri, t, -jnp.inf)
                m = jnp.max(t, axis=-1, keepdims=True)
                p = jnp.exp2(t - m)
                pv = jnp.dot(p.astype(BF16), vb, preferred_element_type=F32)
                den = pv[:, HEAD_DIM:]
                o = pv[:, :HEAD_DIM] / den
                lse2 = m + jnp.log2(den)
                if dil == 1:
                    rows = pl.ds(g * SEQ + blk * SPAN, SPAN)
                else:
                    rows = pl.ds(g * SEQ + blk * SPAN * dil + r, SPAN, stride=dil)
                o_s[rows, :] = o
                l_s[rows, :] = lse2

    ch = 256
    for c in range(SEQ // ch):
        ls = [l_s[g * SEQ + c * ch:g * SEQ + (c + 1) * ch, :] for g in range(N_DIL_GROUPS)]
        os_ = [o_s[g * SEQ + c * ch:g * SEQ + (c + 1) * ch, :] for g in range(N_DIL_GROUPS)]
        m = jnp.maximum(jnp.maximum(ls[0], ls[1]), ls[2])
        es = [jnp.exp2(x - m) for x in ls]
        inv = 1.0 / (es[0] + es[1] + es[2])
        ya = (es[0] * inv) * os_[0] + (es[1] * inv) * os_[1] + (es[2] * inv) * os_[2]
        ya_ref[c * ch:(c + 1) * ch, :] = (ya * gate_ref[c * ch:(c + 1) * ch, :].astype(F32)).astype(BF16)


def _attn_a(zn, zd):
    def spec(head0):
        return pl.BlockSpec((None, None, SEQ, HEAD_DIM), lambda b, h: (b, head0 + h, 0, 0))

    in_specs = [spec(_NAT_HEAD[s]) for s in ("aq", "ak", "av")]
    in_specs += [spec(_DIL_HEAD[(g, s)]) for g in (1, 2) for s in ("aq", "ak", "av")]
    in_specs.append(spec(_NAT_HEAD["ag"]))
    return pl.pallas_call(
        _attn_a_kernel,
        grid=(BATCH, A_HEADS),
        in_specs=in_specs,
        out_specs=pl.BlockSpec((None, SEQ, HEAD_DIM), lambda b, h: (b, 0, h)),
        out_shape=jax.ShapeDtypeStruct((BATCH, SEQ, A_WIDTH), BF16),
        scratch_shapes=[pltpu.VMEM((SEQ, 2 * HEAD_DIM), BF16)] * N_DIL_GROUPS + [
                        pltpu.VMEM((N_DIL_GROUPS * SEQ, HEAD_DIM), F32),
                        pltpu.VMEM((N_DIL_GROUPS * SEQ, HEAD_DIM), F32)],
        compiler_params=_params(("arbitrary", "arbitrary")),
        name="attn_a",
    )(zn, zn, zn, zd, zd, zd, zd, zd, zd, zn)


def _fox_kernel(q_ref, k_ref, v_ref, gate_ref, f_ref, y_ref, vaug_s, s_s, *, t):
    nq = SEQ // t
    vaug_s[:, :HEAD_DIM] = v_ref[...]
    vaug_s[:, HEAD_DIM:] = jnp.ones((SEQ, HEAD_DIM), BF16)
    kw = 2 * t

    shifts = {}

    def chunks(i):
        n_keys = (i + 1) * t
        return [(c0, min(kw, n_keys - c0)) for c0 in range(0, n_keys, kw)]

    def first_pass(i):
        qb = q_ref[i * t:(i + 1) * t, :]
        f2q = f_ref[:, i * t:(i + 1) * t] * LOG2E
        fq = jnp.transpose(jnp.broadcast_to(f2q, (LANES, t)))[:, :1]
        buf = (i % 2) * SEQ
        m = None
        for c0, w in chunks(i):
            kb = k_ref[c0:c0 + w, :]
            f2k = f_ref[:, c0:c0 + w] * LOG2E
            s = lax.dot_general(qb, kb, (((1,), (1,)), ((), ())), preferred_element_type=F32) * C1 - f2k
            if c0 + w == (i + 1) * t:
                row = lax.broadcasted_iota(jnp.int32, (t, w), 0) + i * t
                col = lax.broadcasted_iota(jnp.int32, (t, w), 1) + c0
                s = jnp.where(col <= row, s, -jnp.inf)
            s_s[:, buf + c0:buf + c0 + w] = s
            rm = jnp.max(s, axis=-1, keepdims=True)
            m = rm if m is None else jnp.maximum(m, rm)
            yield
        shifts[i] = fq - (m + fq)

    def second_pass(i):
        buf = (i % 2) * SEQ
        acc = None
        for c0, w in chunks(i):
            p = jnp.exp2(s_s[:, buf + c0:buf + c0 + w] + shifts[i])
            pv = jnp.dot(p.astype(BF16), vaug_s[c0:c0 + w, :], preferred_element_type=F32)
            acc = pv if acc is None else acc + pv
            yield
        o = acc[:, :HEAD_DIM] / acc[:, HEAD_DIM:]
        y_ref[i * t:(i + 1) * t, :] = (o * gate_ref[i * t:(i + 1) * t, :].astype(F32)).astype(BF16)

    for _ in first_pass(0):
        pass
    for i in range(nq):
        live = [second_pass(i)] + ([first_pass(i + 1)] if i + 1 < nq else [])
        while live:
            live = [g for g in live if next(g, StopIteration) is not StopIteration]


def _fox_attention(zn, f_row):
    t = 512
    spec = lambda s: pl.BlockSpec((None, None, SEQ, HEAD_DIM), lambda b, h: (b, _NAT_HEAD[s] + h, 0, 0))
    return pl.pallas_call(
        functools.partial(_fox_kernel, t=t),
        grid=(BATCH, B_HEADS),
        in_specs=[spec("bq"), spec("bk"), spec("bv"), spec("bg"),
                  pl.BlockSpec((None, None, 1, SEQ), lambda b, h: (b, h, 0, 0))],
        out_specs=pl.BlockSpec((None, SEQ, HEAD_DIM), lambda b, h: (b, 0, h)),
        out_shape=jax.ShapeDtypeStruct((BATCH, SEQ, B_WIDTH), BF16),
        scratch_shapes=[pltpu.VMEM((SEQ, 2 * HEAD_DIM), BF16), pltpu.VMEM((t, 2 * SEQ), F32)],
        compiler_params=_params(("arbitrary", "arbitrary")),
        name="fox_attention",
    )(zn, zn, zn, zn, f_row)


def _merge_kernel(ya_ref, yb_ref, wa_ref, wb_ref, ga_ref, gb_ref, o_ref):
    pa = jnp.dot(ya_ref[...], wa_ref[...], preferred_element_type=F32)
    pb = jnp.dot(yb_ref[...], wb_ref[...], preferred_element_type=F32)
    o_ref[...] = (ga_ref[...].astype(F32) * pa + gb_ref[...].astype(F32) * pb).astype(BF16)


def _merge(ya, yb, wa, wb, gates):
    tm, tn = 1024, 1024
    nb = D_MODEL // tn
    return pl.pallas_call(
        _merge_kernel,
        grid=(N_TOK // tm, nb),
        in_specs=[pl.BlockSpec((tm, A_WIDTH), lambda i, j: (i, 0)),
                  pl.BlockSpec((tm, B_WIDTH), lambda i, j: (i, 0)),
                  pl.BlockSpec((A_WIDTH, tn), lambda i, j: (0, j)),
                  pl.BlockSpec((B_WIDTH, tn), lambda i, j: (0, j)),
                  pl.BlockSpec((tm, tn), lambda i, j: (i, j)),
                  pl.BlockSpec((tm, tn), lambda i, j: (i, nb + j))],
        out_specs=pl.BlockSpec((tm, tn), lambda i, j: (i, j)),
        out_shape=jax.ShapeDtypeStruct((N_TOK, D_MODEL), BF16),
        compiler_params=_params(("arbitrary", "arbitrary")),
        name="merge",
    )(ya, yb, wa, wb, gates, gates)


def _out_kernel(m_ref, w_ref, x_ref, gate_ref, o_ref):
    out = jnp.dot(m_ref[...], w_ref[...], preferred_element_type=F32)
    o_ref[...] = x_ref[...] + gate_ref[...] * out


def _out_proj(merged, w_o, x2, gate):
    tm, tn = 1024, 1024
    per_b = SEQ // tm
    return pl.pallas_call(
        _out_kernel,
        grid=(N_TOK // tm, D_MODEL // tn),
        in_specs=[pl.BlockSpec((tm, D_MODEL), lambda i, j: (i, 0)),
                  pl.BlockSpec((D_MODEL, tn), lambda i, j: (0, j)),
                  pl.BlockSpec((tm, tn), lambda i, j: (i, j)),
                  pl.BlockSpec((None, 1, tn), lambda i, j: (i // per_b, 0, j))],
        out_specs=pl.BlockSpec((tm, tn), lambda i, j: (i, j)),
        out_shape=jax.ShapeDtypeStruct((N_TOK, D_MODEL), F32),
        compiler_params=_params(("arbitrary", "arbitrary")),
        name="out_proj",
    )(merged, w_o, x2, gate.reshape(BATCH, 1, D_MODEL))


def _rope_tables():
    half = HEAD_DIM // 2
    inv = ROPE_THETA ** (-jnp.arange(half, dtype=F32) / half)
    ang = jnp.arange(SEQ).astype(F32)[:, None] * inv[None, :]
    cos, sin = jnp.cos(ang), jnp.sin(ang)
    cos = jnp.concatenate([cos, cos], axis=-1)
    sin = jnp.concatenate([-sin, sin], axis=-1)

    def perm(t, dil):
        t = t.reshape(SEQ // PROJ_TM, PROJ_TM // dil, dil, HEAD_DIM)
        return jnp.transpose(t, (0, 2, 1, 3)).reshape(SEQ, HEAD_DIM)

    return (jnp.stack([perm(cos, d) for d in DILS] + [jnp.ones_like(cos)]),
            jnp.stack([perm(sin, d) for d in DILS] + [jnp.zeros_like(sin)]))


def _head_rows(aq, ak, bq, bk):
    ones = jnp.ones((HEAD_DIM,), F32)
    gains = ([aq[g] for g in range(N_DIL_GROUPS) for _ in range(A_HEADS)]
             + [ak[g] for g in range(N_DIL_GROUPS) for _ in range(A_HEADS)]
             + [ones] * ((N_DIL_GROUPS + 1) * A_HEADS)
             + [bq] * B_HEADS + [bk] * B_HEADS + [ones] * (2 * B_HEADS))
    flag = np.zeros((B_F0,), np.float32)
    flag[A_Q0:A_V0] = 1.0
    flag[A_G0:B_Q0] = 1.0
    flag[B_G0:B_F0] = 1.0
    return jnp.concatenate(gains).reshape(1, B_F0), jnp.asarray(flag).reshape(1, B_F0)


def kernel(x, c, norm_g, w_ada, b_ada, w_in, b_in, a_q_norm, a_k_norm, b_q_norm, b_k_norm,
           w_a_out, w_b_out, w_o):
    depth = norm_g.shape[0]
    cos_t, sin_t = _rope_tables()
    nat_meta = jnp.asarray(np.array([p + (0,) for p in _NAT_PLAN], np.int32).T)
    dil_meta = jnp.asarray(np.array([p + (0,) for p in _DIL_PLAN], np.int32).T)
    xs = x.reshape(N_TOK, D_MODEL)
    for l in range(depth):
        mod = _modulation(c, w_ada[l], b_ada[l])
        shift, scale, gate = mod[:, :D_MODEL], mod[:, D_MODEL:2 * D_MODEL], mod[:, 2 * D_MODEL:]
        w_t = jnp.swapaxes(w_in[l], 0, 1).astype(BF16)
        h, f_cum = _norm_modulate(xs, norm_g[l], scale, shift, w_t, b_in[l, B_F0:B_F0 + LANES].reshape(1, LANES))
        w_heads, b_heads = w_t, b_in[l, :B_F0].reshape(1, B_F0)
        gains, flags = _head_rows(a_q_norm[l], a_k_norm[l], b_q_norm[l], b_k_norm[l])
        zn = _project_heads(h, w_heads, b_heads, gains, flags, cos_t, sin_t, nat_meta,
                            (("qk", 1), ("act", 1)), "proj_heads_nat")
        zd = _project_heads(h, w_heads, b_heads, gains, flags, cos_t, sin_t, dil_meta,
                            (("gen", DILS[1]), ("gen", DILS[2])), "proj_heads_dil")
        gates = _project_gates(h, w_t, b_in[l, M_G0:])

        ya = _attn_a(zn, zd)
        f_row = jnp.transpose(f_cum.reshape(BATCH, SEQ, LANES)[:, :, :B_HEADS], (0, 2, 1))[:, :, None, :]
        yb = _fox_attention(zn, f_row)

        merged = _merge(ya.reshape(N_TOK, A_WIDTH), yb.reshape(N_TOK, B_WIDTH),
                        w_a_out[l].astype(BF16), w_b_out[l].astype(BF16), gates)
        xs = _out_proj(merged, w_o[l].astype(BF16), xs, gate)
    return xs.reshape(BATCH, SEQ, D_MODEL)
```

```python
import functools
import math

import jax
import jax.numpy as jnp
import numpy as np
from jax import lax
from jax.experimental import pallas as pl
from jax.experimental.pallas import tpu as pltpu

F32 = jnp.float32
BF16 = jnp.bfloat16

D_MODEL = 4096
BATCH = 4
SEQ = 4096
HEAD_DIM = 128
ROPE_THETA = 10000.0
EPS = 1e-6
DIL_PATTERNS = ((128, 1), (512, 4), (2048, 16))
DILS = tuple(d for _, d in DIL_PATTERNS)
N_DIL_GROUPS = 3
A_HEADS = 8
A_WIDTH = A_HEADS * HEAD_DIM
A_QKV_WIDTH = N_DIL_GROUPS * A_WIDTH
B_HEADS = 16
B_WIDTH = B_HEADS * HEAD_DIM
A_Q0 = 0
A_K0 = A_Q0 + A_QKV_WIDTH
A_V0 = A_K0 + A_QKV_WIDTH
A_G0 = A_V0 + A_QKV_WIDTH
B_Q0 = A_G0 + A_WIDTH
B_K0 = B_Q0 + B_WIDTH
B_V0 = B_K0 + B_WIDTH
B_G0 = B_V0 + B_WIDTH
B_F0 = B_G0 + B_WIDTH
M_G0 = B_F0 + B_HEADS
IN_COLS = M_G0 + 2 * D_MODEL
N_TOK = BATCH * SEQ
SPAN = 128
LANES = 128
VMEM_LIMIT = 56 * 1024 * 1024
PROJ_TM = 1024
PROJ_TN = 1024
HEADS_PER_BLOCK = PROJ_TN // HEAD_DIM
LOG2E = math.log2(math.e)
C1 = (HEAD_DIM ** -0.5) * LOG2E


def _params(sem, vmem=VMEM_LIMIT):
    return pltpu.CompilerParams(dimension_semantics=sem, vmem_limit_bytes=vmem)


def _dot_nt(a, bt):
    return lax.dot_general(a, bt, (((1,), (1,)), ((), ())), preferred_element_type=F32)


def _sigmoid(x):
    return 0.5 * jnp.tanh(0.5 * x) + 0.5


def _silu(x):
    return x * _sigmoid(x)


def _mod_kernel(c_ref, w_ref, b_ref, o_ref):
    s = _silu(c_ref[...]).astype(BF16)
    o_ref[...] = jnp.dot(s, w_ref[...].astype(BF16), preferred_element_type=F32) + b_ref[...]


def _modulation(c, w_ada, b_ada):
    tn = 512
    n = w_ada.shape[1]
    cp = jnp.pad(c, ((0, 8 - BATCH), (0, 0)))
    out = pl.pallas_call(
        _mod_kernel,
        grid=(n // tn,),
        in_specs=[pl.BlockSpec((8, D_MODEL), lambda j: (0, 0)),
                  pl.BlockSpec((D_MODEL, tn), lambda j: (0, j)),
                  pl.BlockSpec((1, tn), lambda j: (0, j))],
        out_specs=pl.BlockSpec((8, tn), lambda j: (0, j)),
        out_shape=jax.ShapeDtypeStruct((8, n), F32),
        compiler_params=_params(("arbitrary",)),
        name="modulation",
    )(cp, w_ada, b_ada.reshape(1, n))
    return out[:BATCH]


def _norm_kernel(x_ref, g_ref, scale_ref, shift_ref, wf_ref, bf_ref, h_ref, f_ref, carry_ref, *, tm, per_b):
    x = x_ref[...]
    ms = jnp.mean(x * x, axis=-1, keepdims=True)
    y = x * lax.rsqrt(ms + EPS) * g_ref[...]
    h = (y * (1.0 + scale_ref[...]) + shift_ref[...]).astype(BF16)
    h_ref[...] = h

    @pl.when(pl.program_id(0) % per_b == 0)
    def _():
        carry_ref[...] = jnp.zeros_like(carry_ref)

    z = _dot_nt(h, wf_ref[...]) + bf_ref[...]
    lf = jnp.minimum(z, 0.0) - jnp.log1p(jnp.exp(-jnp.abs(z)))
    row = lax.broadcasted_iota(jnp.int32, lf.shape, 0)
    shift = 1
    while shift < tm:
        lf = lf + jnp.where(row >= shift, pltpu.roll(lf, shift, 0), 0.0)
        shift *= 2
    lf = lf + carry_ref[...]
    f_ref[...] = lf
    carry_ref[...] = lf[tm - 1:tm, :]


def _norm_modulate(x2, g, scale, shift, w_t, b_f):
    tm = 512
    per_b = SEQ // tm
    return pl.pallas_call(
        functools.partial(_norm_kernel, tm=tm, per_b=per_b),
        grid=(N_TOK // tm,),
        in_specs=[pl.BlockSpec((tm, D_MODEL), lambda i: (i, 0)),
                  pl.BlockSpec((1, D_MODEL), lambda i: (0, 0)),
                  pl.BlockSpec((None, 1, D_MODEL), lambda i: (i // per_b, 0, 0)),
                  pl.BlockSpec((None, 1, D_MODEL), lambda i: (i // per_b, 0, 0)),
                  pl.BlockSpec((LANES, D_MODEL), lambda i: (B_F0 // LANES, 0)),
                  pl.BlockSpec((1, LANES), lambda i: (0, 0))],
        out_specs=[pl.BlockSpec((tm, D_MODEL), lambda i: (i, 0)),
                   pl.BlockSpec((tm, LANES), lambda i: (i, 0))],
        out_shape=[jax.ShapeDtypeStruct((N_TOK, D_MODEL), BF16),
                   jax.ShapeDtypeStruct((N_TOK, LANES), F32)],
        scratch_shapes=[pltpu.VMEM((1, LANES), F32)],
        compiler_params=_params(("arbitrary",)),
        name="norm_modulate",
    )(x2, g.reshape(1, D_MODEL), scale.reshape(BATCH, 1, D_MODEL), shift.reshape(BATCH, 1, D_MODEL), w_t, b_f)


def _head_proj_kernel(meta_ref, h_ref, w_ref, b_ref, gain_ref, flag_ref, cos_ref, sin_ref, o_ref, perm_s,
                      *, bodies):
    j = pl.program_id(1)
    tm = h_ref.shape[0]
    chunk = 256

    def epilogue(x, hh, rows, kind):
        lanes = slice(hh * HEAD_DIM, (hh + 1) * HEAD_DIM)
        if kind == "act":
            return jnp.where(flag_ref[:, lanes] > 0.5, _silu(x), x).astype(BF16)
        ms = jnp.mean(x * x, axis=-1, keepdims=True)
        y = x * lax.rsqrt(ms + EPS) * gain_ref[:, lanes]
        if kind == "norm":
            return y.astype(BF16)
        if kind == "gen":
            y = jnp.where(flag_ref[:, lanes] > 0.5, y, x)
        y = y * cos_ref[rows, :] + pltpu.roll(y, HEAD_DIM // 2, 1) * sin_ref[rows, :]
        return y.astype(BF16)

    def run(kind, dil):
        acc = _dot_nt(h_ref[...], w_ref[...]) + b_ref[...]
        for hh in range(HEADS_PER_BLOCK):
            lanes = slice(hh * HEAD_DIM, (hh + 1) * HEAD_DIM)
            if kind in ("act", "norm"):
                for c0 in range(0, tm, chunk):
                    rows = slice(c0, c0 + chunk)
                    o_ref[hh, rows, :] = epilogue(acc[rows, lanes], hh, rows, kind)
                continue
            perm_s[hh * tm:(hh + 1) * tm, :] = acc[:, lanes]
            if dil == 1:
                for c0 in range(0, tm, chunk):
                    rows = slice(c0, c0 + chunk)
                    start = pl.multiple_of(hh * tm + c0 + meta_ref[3, j], chunk)
                    o_ref[hh, rows, :] = epilogue(perm_s[pl.ds(start, chunk), :], hh, rows, kind)
            else:
                n = tm // dil
                for r in range(dil):
                    rows = slice(r * n, (r + 1) * n)
                    xr = perm_s[pl.ds(hh * tm + r, n, stride=dil), :]
                    o_ref[hh, rows, :] = epilogue(xr, hh, rows, kind)

    for bid, (kind, dil) in enumerate(bodies):
        pl.when(meta_ref[2, j] == bid)(functools.partial(run, kind, dil))


def _project_heads(h, w, b, gain, flag, cos_t, sin_t, meta, bodies, name):
    tm, tn = PROJ_TM, PROJ_TN
    nj = meta.shape[1]
    per_b = SEQ // tm
    col = lambda i, j, m: (0, m[0, j])
    wrow = lambda i, j, m: (m[0, j], 0)
    tab = lambda i, j, m: (m[1, j], i % per_b, 0)
    grid_spec = pltpu.PrefetchScalarGridSpec(
        num_scalar_prefetch=1,
        grid=(N_TOK // tm, nj),
        in_specs=[pl.BlockSpec((tm, D_MODEL), lambda i, j, m: (i, 0)),
                  pl.BlockSpec((tn, D_MODEL), wrow),
                  pl.BlockSpec((1, tn), col),
                  pl.BlockSpec((1, tn), col),
                  pl.BlockSpec((1, tn), col),
                  pl.BlockSpec((None, tm, HEAD_DIM), tab),
                  pl.BlockSpec((None, tm, HEAD_DIM), tab)],
        out_specs=pl.BlockSpec((None, HEADS_PER_BLOCK, tm, HEAD_DIM),
                               lambda i, j, m: (i // per_b, j, i % per_b, 0)),
        scratch_shapes=[pltpu.VMEM((HEADS_PER_BLOCK * tm, HEAD_DIM), F32)],
    )
    return pl.pallas_call(
        functools.partial(_head_proj_kernel, bodies=bodies),
        grid_spec=grid_spec,
        out_shape=jax.ShapeDtypeStruct((BATCH, nj * HEADS_PER_BLOCK, SEQ, HEAD_DIM), BF16),
        compiler_params=_params(("arbitrary", "arbitrary")),
        name=name,
    )(meta, h, w, b, gain, flag, cos_t, sin_t)


_BLK = {"aq": 0, "ak": 3, "av": 6, "ag": 9, "bq": 10, "bk": 12, "bv": 14, "bg": 16}
_TAB_ID = 3
_NAT_PLAN = ([(_BLK["bq"] + i, _TAB_ID, 0) for i in range(2)] + [(_BLK["bk"] + i, _TAB_ID, 0) for i in range(2)]
             + [(_BLK["av"], _TAB_ID, 1), (_BLK["ag"], _TAB_ID, 1)]
             + [(_BLK["bv"] + i, _TAB_ID, 1) for i in range(2)] + [(_BLK["bg"] + i, _TAB_ID, 1) for i in range(2)])
_ROPE_PLAN = [(_BLK["aq"], 0, 0), (_BLK["ak"], 0, 0)]
_DIL_PLAN = [(_BLK[s] + g, (g if s != "av" else _TAB_ID), g - 1) for g in (1, 2) for s in ("aq", "ak", "av")]
_NAT_HEAD = {"bq": 0, "bk": 16, "av": 32, "ag": 40, "bv": 48, "bg": 64}
_ROPE_HEAD = {"aq": 0, "ak": 8}
_DIL_HEAD = {(1, "aq"): 0, (1, "ak"): 8, (1, "av"): 16, (2, "aq"): 24, (2, "ak"): 32, (2, "av"): 40}


def _gate_proj_kernel(h_ref, w_ref, b_ref, o_ref):
    z = _dot_nt(h_ref[...], w_ref[...]) + b_ref[...]
    o_ref[...] = _sigmoid(z).astype(BF16)


def _project_gates(h, w, b):
    tm, tn = PROJ_TM, PROJ_TN
    n = 2 * D_MODEL
    return pl.pallas_call(
        _gate_proj_kernel,
        grid=(N_TOK // tm, n // tn),
        in_specs=[pl.BlockSpec((tm, D_MODEL), lambda i, j: (i, 0)),
                  pl.BlockSpec((pl.Element(tn), pl.Element(D_MODEL)),
                               lambda i, j: (pl.multiple_of(M_G0 + j * tn, B_HEADS), 0)),
                  pl.BlockSpec((1, tn), lambda i, j: (0, j))],
        out_specs=pl.BlockSpec((tm, tn), lambda i, j: (i, j)),
        out_shape=jax.ShapeDtypeStruct((N_TOK, n), BF16),
        compiler_params=_params(("arbitrary", "arbitrary")),
        name="proj_gates",
    )(h, w, b.reshape(1, n))


def _perm_rows(dil, r, l0, n):
    per_tile = PROJ_TM // dil
    pieces = []
    while n > 0:
        c, off = divmod(l0, per_tile)
        take = min(n, per_tile - off)
        pieces.append((c * PROJ_TM + r * per_tile + off, take))
        l0 += take
        n -= take
    return pieces


def _load_rows(ref, pieces):
    parts = [ref[s:s + n, :] for s, n in pieces]
    return parts[0] if len(parts) == 1 else jnp.concatenate(parts, axis=0)


def _attn_a_kernel(q0, k0, v0, q1, k1, v1, q2, k2, v2, gate_ref, ya_ref, va0, va1, va2, o_s, l_s):
    qi = lax.broadcasted_iota(jnp.int32, (SPAN, 2 * SPAN), 0)
    kj = lax.broadcasted_iota(jnp.int32, (SPAN, 2 * SPAN), 1)
    dist = qi - kj + SPAN
    band = (dist >= 0) & (dist <= SPAN)
    tri = (lax.broadcasted_iota(jnp.int32, (SPAN, SPAN), 0)
           >= lax.broadcasted_iota(jnp.int32, (SPAN, SPAN), 1))

    for g, (q_ref, k_ref, v_ref, vaug_s) in enumerate(((q0, k0, v0, va0), (q1, k1, v1, va1), (q2, k2, v2, va2))):
        dil = DILS[g]
        nb = SEQ // dil // SPAN
        vaug_s[:, HEAD_DIM:] = jnp.ones((SEQ, HEAD_DIM), BF16)
        vaug_s[:, :HEAD_DIM] = v_ref[...]
        for r in range(dil):
            for blk in range(nb):
                cur = _perm_rows(dil, r, blk * SPAN, SPAN)
                keys = (_perm_rows(dil, r, (blk - 1) * SPAN, SPAN) if blk > 0 else []) + cur
                qb = _load_rows(q_ref, cur)
                kb = _load_rows(k_ref, keys)
                vb = _load_rows(vaug_s, keys)
                t = lax.dot_general(qb, kb, (((1,), (1,)), ((), ())), preferred_element_type=F32) * C1
                t = jnp.where(band if blk > 0 else tri, t, -jnp.inf)
                m = jnp.max(t, axis=-1, keepdims=True)
                p = jnp.exp2(t - m)
                pv = jnp.dot(p.astype(BF16), vb, preferred_element_type=F32)
                den = pv[:, HEAD_DIM:]
                o = pv[:, :HEAD_DIM] / den
                lse2 = m + jnp.log2(den)
                if dil == 1:
                    rows = pl.ds(g * SEQ + blk * SPAN, SPAN)
                else:
                    rows = pl.ds(g * SEQ + blk * SPAN * dil + r, SPAN, stride=dil)
                o_s[rows, :] = o
                l_s[rows, :] = lse2

    ch = 256
    for c in range(SEQ // ch):
        ls = [l_s[g * SEQ + c * ch:g * SEQ + (c + 1) * ch, :] for g in range(N_DIL_GROUPS)]
        os_ = [o_s[g * SEQ + c * ch:g * SEQ + (c + 1) * ch, :] for g in range(N_DIL_GROUPS)]
        m = jnp.maximum(jnp.maximum(ls[0], ls[1]), ls[2])
        es = [jnp.exp2(x - m) for x in ls]
        inv = 1.0 / (es[0] + es[1] + es[2])
        ya = (es[0] * inv) * os_[0] + (es[1] * inv) * os_[1] + (es[2] * inv) * os_[2]
        ya_ref[c * ch:(c + 1) * ch, :] = (ya * gate_ref[c * ch:(c + 1) * ch, :].astype(F32)).astype(BF16)


def _attn_a(zn, zd, zr):
    def spec(head0):
        return pl.BlockSpec((None, None, SEQ, HEAD_DIM), lambda b, h: (b, head0 + h, 0, 0))

    in_specs = [spec(_ROPE_HEAD["aq"]), spec(_ROPE_HEAD["ak"]), spec(_NAT_HEAD["av"])]
    in_specs += [spec(_DIL_HEAD[(g, s)]) for g in (1, 2) for s in ("aq", "ak", "av")]
    in_specs.append(spec(_NAT_HEAD["ag"]))
    return pl.pallas_call(
        _attn_a_kernel,
        grid=(BATCH, A_HEADS),
        in_specs=in_specs,
        out_specs=pl.BlockSpec((None, SEQ, HEAD_DIM), lambda b, h: (b, 0, h)),
        out_shape=jax.ShapeDtypeStruct((BATCH, SEQ, A_WIDTH), BF16),
        scratch_shapes=[pltpu.VMEM((SEQ, 2 * HEAD_DIM), BF16)] * N_DIL_GROUPS + [
                        pltpu.VMEM((N_DIL_GROUPS * SEQ, HEAD_DIM), F32),
                        pltpu.VMEM((N_DIL_GROUPS * SEQ, HEAD_DIM), F32)],
        compiler_params=_params(("arbitrary", "arbitrary")),
        name="attn_a",
    )(zr, zr, zn, zd, zd, zd, zd, zd, zd, zn)


def _fox_kernel(q_ref, k_ref, v_ref, gate_ref, f_ref, y_ref, vaug_s, s_s, *, t):
    nq = SEQ // t
    vaug_s[:, :HEAD_DIM] = v_ref[...]
    vaug_s[:, HEAD_DIM:] = jnp.ones((SEQ, HEAD_DIM), BF16)
    kw = 2 * t

    shifts = {}

    def chunks(i):
        n_keys = (i + 1) * t
        return [(c0, min(kw, n_keys - c0)) for c0 in range(0, n_keys, kw)]

    def first_pass(i):
        qb = q_ref[i * t:(i + 1) * t, :]
        f2q = f_ref[:, i * t:(i + 1) * t] * LOG2E
        fq = jnp.transpose(jnp.broadcast_to(f2q, (LANES, t)))[:, :1]
        buf = (i % 2) * SEQ
        m = None
        for c0, w in chunks(i):
            kb = k_ref[c0:c0 + w, :]
            f2k = f_ref[:, c0:c0 + w] * LOG2E
            s = lax.dot_general(qb, kb, (((1,), (1,)), ((), ())), preferred_element_type=F32) * C1 - f2k
            if c0 + w == (i + 1) * t:
                row = lax.broadcasted_iota(jnp.int32, (t, w), 0) + i * t
                col = lax.broadcasted_iota(jnp.int32, (t, w), 1) + c0
                s = jnp.where(col <= row, s, -jnp.inf)
            s_s[:, buf + c0:buf + c0 + w] = s
            rm = jnp.max(s, axis=-1, keepdims=True)
            m = rm if m is None else jnp.maximum(m, rm)
            yield
        shifts[i] = fq - (m + fq)

    def second_pass(i):
        buf = (i % 2) * SEQ
        acc = None
        for c0, w in chunks(i):
            p = jnp.exp2(s_s[:, buf + c0:buf + c0 + w] + shifts[i])
            pv = jnp.dot(p.astype(BF16), vaug_s[c0:c0 + w, :], preferred_element_type=F32)
            acc = pv if acc is None else acc + pv
            yield
        o = acc[:, :HEAD_DIM] / acc[:, HEAD_DIM:]
        y_ref[i * t:(i + 1) * t, :] = (o * gate_ref[i * t:(i + 1) * t, :].astype(F32)).astype(BF16)

    for _ in first_pass(0):
        pass
    for i in range(nq):
        live = [second_pass(i)] + ([first_pass(i + 1)] if i + 1 < nq else [])
        while live:
            live = [g for g in live if next(g, StopIteration) is not StopIteration]


def _fox_attention(zn, f_row):
    t = 512
    spec = lambda s: pl.BlockSpec((None, None, SEQ, HEAD_DIM), lambda b, h: (b, _NAT_HEAD[s] + h, 0, 0))
    return pl.pallas_call(
        functools.partial(_fox_kernel, t=t),
        grid=(BATCH, B_HEADS),
        in_specs=[spec("bq"), spec("bk"), spec("bv"), spec("bg"),
                  pl.BlockSpec((None, None, 1, SEQ), lambda b, h: (b, h, 0, 0))],
        out_specs=pl.BlockSpec((None, SEQ, HEAD_DIM), lambda b, h: (b, 0, h)),
        out_shape=jax.ShapeDtypeStruct((BATCH, SEQ, B_WIDTH), BF16),
        scratch_shapes=[pltpu.VMEM((SEQ, 2 * HEAD_DIM), BF16), pltpu.VMEM((t, 2 * SEQ), F32)],
        compiler_params=_params(("arbitrary", "arbitrary")),
        name="fox_attention",
    )(zn, zn, zn, zn, f_row)


def _merge_kernel(ya_ref, yb_ref, wa_ref, wb_ref, ga_ref, gb_ref, o_ref):
    pa = jnp.dot(ya_ref[...], wa_ref[...], preferred_element_type=F32)
    pb = jnp.dot(yb_ref[...], wb_ref[...], preferred_element_type=F32)
    o_ref[...] = (ga_ref[...].astype(F32) * pa + gb_ref[...].astype(F32) * pb).astype(BF16)


def _merge(ya, yb, wa, wb, gates):
    tm, tn = 1024, 1024
    nb = D_MODEL // tn
    return pl.pallas_call(
        _merge_kernel,
        grid=(N_TOK // tm, nb),
        in_specs=[pl.BlockSpec((tm, A_WIDTH), lambda i, j: (i, 0)),
                  pl.BlockSpec((tm, B_WIDTH), lambda i, j: (i, 0)),
                  pl.BlockSpec((A_WIDTH, tn), lambda i, j: (0, j)),
                  pl.BlockSpec((B_WIDTH, tn), lambda i, j: (0, j)),
                  pl.BlockSpec((tm, tn), lambda i, j: (i, j)),
                  pl.BlockSpec((tm, tn), lambda i, j: (i, nb + j))],
        out_specs=pl.BlockSpec((tm, tn), lambda i, j: (i, j)),
        out_shape=jax.ShapeDtypeStruct((N_TOK, D_MODEL), BF16),
        compiler_params=_params(("arbitrary", "arbitrary")),
        name="merge",
    )(ya, yb, wa, wb, gates, gates)


def _out_kernel(m_ref, w_ref, x_ref, gate_ref, o_ref):
    out = jnp.dot(m_ref[...], w_ref[...], preferred_element_type=F32)
    o_ref[...] = x_ref[...] + gate_ref[...] * out


def _out_proj(merged, w_o, x2, gate):
    tm, tn = 1024, 1024
    per_b = SEQ // tm
    return pl.pallas_call(
        _out_kernel,
        grid=(N_TOK // tm, D_MODEL // tn),
        in_specs=[pl.BlockSpec((tm, D_MODEL), lambda i, j: (i, 0)),
                  pl.BlockSpec((D_MODEL, tn), lambda i, j: (0, j)),
                  pl.BlockSpec((tm, tn), lambda i, j: (i, j)),
                  pl.BlockSpec((None, 1, tn), lambda i, j: (i // per_b, 0, j))],
        out_specs=pl.BlockSpec((tm, tn), lambda i, j: (i, j)),
        out_shape=jax.ShapeDtypeStruct((N_TOK, D_MODEL), F32),
        compiler_params=_params(("arbitrary", "arbitrary")),
        name="out_proj",
    )(merged, w_o, x2, gate.reshape(BATCH, 1, D_MODEL))


def _rope_tables():
    half = HEAD_DIM // 2
    inv = ROPE_THETA ** (-jnp.arange(half, dtype=F32) / half)
    ang = jnp.arange(SEQ).astype(F32)[:, None] * inv[None, :]
    cos, sin = jnp.cos(ang), jnp.sin(ang)
    cos = jnp.concatenate([cos, cos], axis=-1)
    sin = jnp.concatenate([-sin, sin], axis=-1)

    def perm(t, dil):
        t = t.reshape(SEQ // PROJ_TM, PROJ_TM // dil, dil, HEAD_DIM)
        return jnp.transpose(t, (0, 2, 1, 3)).reshape(SEQ, HEAD_DIM)

    return (jnp.stack([perm(cos, d) for d in DILS] + [jnp.ones_like(cos)]),
            jnp.stack([perm(sin, d) for d in DILS] + [jnp.zeros_like(sin)]))


def _head_rows(aq, ak, bq, bk):
    ones = jnp.ones((HEAD_DIM,), F32)
    gains = ([aq[g] for g in range(N_DIL_GROUPS) for _ in range(A_HEADS)]
             + [ak[g] for g in range(N_DIL_GROUPS) for _ in range(A_HEADS)]
             + [ones] * ((N_DIL_GROUPS + 1) * A_HEADS)
             + [bq] * B_HEADS + [bk] * B_HEADS + [ones] * (2 * B_HEADS))
    flag = np.zeros((B_F0,), np.float32)
    flag[A_Q0:A_V0] = 1.0
    flag[A_G0:B_Q0] = 1.0
    flag[B_G0:B_F0] = 1.0
    return jnp.concatenate(gains).reshape(1, B_F0), jnp.asarray(flag).reshape(1, B_F0)


def kernel(x, c, norm_g, w_ada, b_ada, w_in, b_in, a_q_norm, a_k_norm, b_q_norm, b_k_norm,
           w_a_out, w_b_out, w_o):
    depth = norm_g.shape[0]
    cos_t, sin_t = _rope_tables()
    nat_meta = jnp.asarray(np.array([p + (0,) for p in _NAT_PLAN], np.int32).T)
    dil_meta = jnp.asarray(np.array([p + (0,) for p in _DIL_PLAN], np.int32).T)
    rope_meta = jnp.asarray(np.array([p + (0,) for p in _ROPE_PLAN], np.int32).T)
    xs = x.reshape(N_TOK, D_MODEL)
    for l in range(depth):
        mod = _modulation(c, w_ada[l], b_ada[l])
        shift, scale, gate = mod[:, :D_MODEL], mod[:, D_MODEL:2 * D_MODEL], mod[:, 2 * D_MODEL:]
        w_t = jnp.swapaxes(w_in[l], 0, 1).astype(BF16)
        h, f_cum = _norm_modulate(xs, norm_g[l], scale, shift, w_t, b_in[l, B_F0:B_F0 + LANES].reshape(1, LANES))
        w_heads, b_heads = w_t, b_in[l, :B_F0].reshape(1, B_F0)
        gains, flags = _head_rows(a_q_norm[l], a_k_norm[l], b_q_norm[l], b_k_norm[l])
        zn = _project_heads(h, w_heads, b_heads, gains, flags, cos_t, sin_t, nat_meta,
                            (("norm", 1), ("act", 1)), "proj_heads_nat")
        zr = _project_heads(h, w_heads, b_heads, gains, flags, cos_t, sin_t, rope_meta,
                            (("qk", 1),), "proj_heads_rope")
        zd = _project_heads(h, w_heads, b_heads, gains, flags, cos_t, sin_t, dil_meta,
                            (("gen", DILS[1]), ("gen", DILS[2])), "proj_heads_dil")
        gates = _project_gates(h, w_t, b_in[l, M_G0:])

        ya = _attn_a(zn, zd, zr)
        f_row = jnp.transpose(f_cum.reshape(BATCH, SEQ, LANES)[:, :, :B_HEADS], (0, 2, 1))[:, :, None, :]
        yb = _fox_attention(zn, f_row)

        merged = _merge(ya.reshape(N_TOK, A_WIDTH), yb.reshape(N_TOK, B_WIDTH),
                        w_a_out[l].astype(BF16), w_b_out[l].astype(BF16), gates)
        xs = _out_proj(merged, w_o[l].astype(BF16), xs, gate)
    return xs.reshape(BATCH, SEQ, D_MODEL)
```
